```python
import math
import jax, jax.numpy as jnp
from jax import lax
import numpy as np

D_MODEL = 1024
BATCH = 32
SEQ = 2048
DEPTH = 2

H_A = 4
DH_A = 64
W_A = H_A * 2 * DH_A
Q_BLOCK = 128
ROPE_THETA = 10000.0
H_B = 8
N_B = 64
W_B = H_B * N_B
R_W = 64
R_A = 64
R_G = 160
RWKV_LN_EPS = 64e-5
H_C = 4
DK_C = 128
DV_C = 128
W_C = H_C * DK_C
HGRN_CHUNK = 32
D_FF = 2816
N_MOD = 9
EPS = 1e-6

P_A = 3 * W_A
P_B = 3 * W_B + R_W + R_A + R_G
P_C = 4 * W_C
P_G = 3 * D_MODEL
P_TOTAL = P_A + P_B + P_C + P_G

kernel_name = 'hybrid_diffattn_rwkv7_hgrn2_block'


def rms_norm(x, w, eps=EPS):
    xf = x.astype(jnp.float32)
    y = xf * lax.rsqrt(jnp.mean(xf * xf, axis=-1, keepdims=True) + eps)
    return (y * w.astype(jnp.float32)).astype(x.dtype)


def modulate(n, shift, scale):
    return n * (1.0 + scale) + shift


def swiglu(u, w_gate, w_up, w_down):
    return (jax.nn.silu(u @ w_gate) * (u @ w_up)) @ w_down


def rope(t, positions):
    dh = t.shape[-1]
    inv = ROPE_THETA ** (-jnp.arange(0, dh, 2, dtype=jnp.float32) / dh)
    ang = positions.astype(jnp.float32)[..., None] * inv
    cos = jnp.cos(ang)[:, :, None, None, :]
    sin = jnp.sin(ang)[:, :, None, None, :]
    tf = t.astype(jnp.float32)
    t1, t2 = tf[..., : dh // 2], tf[..., dh // 2:]
    return jnp.concatenate([t1 * cos - t2 * sin, t2 * cos + t1 * sin], axis=-1).astype(t.dtype)


def diff_attention(q, k, v, positions, qk_norm_w, lambda_qk, subln_w, layer_idx):
    B, S = q.shape[0], q.shape[1]
    q = rope(rms_norm(q, qk_norm_w[0]), positions)
    k = rope(rms_norm(k, qk_norm_w[1]), positions)
    lam_init = 0.8 - 0.6 * math.exp(-0.3 * layer_idx)
    lq = lambda_qk.astype(jnp.float32)
    lam = jnp.exp(jnp.sum(lq[0] * lq[1])) - jnp.exp(jnp.sum(lq[2] * lq[3])) + lam_init
    scale = DH_A ** -0.5
    outs = []
    for blk in range(S // Q_BLOCK):
        q0 = blk * Q_BLOCK
        kv_len = q0 + Q_BLOCK
        qb = q[:, q0:kv_len]
        kb = k[:, :kv_len]
        vb = v[:, :kv_len]
        s = jnp.einsum('bqhmd,bkhmd->bhmqk', qb, kb, preferred_element_type=jnp.float32) * scale
        causal = jnp.arange(kv_len)[None, :] <= (q0 + jnp.arange(Q_BLOCK))[:, None]
        p = jax.nn.softmax(jnp.where(causal, s, -jnp.inf), axis=-1)
        p = p[:, :, 0] - lam * p[:, :, 1]
        outs.append(jnp.einsum('bhqk,bkhe->bqhe', p.astype(v.dtype), vb))
    o = jnp.concatenate(outs, axis=1)
    o = rms_norm(o, subln_w) * (1.0 - lam_init)
    return o.reshape(B, S, W_A)


def token_shift(p):
    return jnp.pad(p, ((0, 0), (1, 0), (0, 0)))[:, :-1]


def rwkv7_scan(r, decay, k, v, kk, a):
    B, _, H, N = r.shape

    def step(state, inp):
        r_t, w_t, k_t, v_t, kk_t, a_t = inp
        s_kk = jnp.einsum('bhvk,bhk->bhv', state, kk_t)
        state = (state * w_t[:, :, None, :]
                 - s_kk[..., None] * (kk_t * a_t)[:, :, None, :]
                 + v_t[..., None] * k_t[:, :, None, :])
        return state, jnp.einsum('bhvk,bhk->bhv', state, r_t)

    xs = (jnp.moveaxis(t, 1, 0) for t in (r, decay, k, v, kk, a))
    _, out = lax.scan(step, jnp.zeros((B, H, N, N), jnp.float32), tuple(xs))
    return jnp.moveaxis(out, 0, 1)


def rwkv7_time_mix(p, mu, w0, w2, a0, a2, g2, k_k, k_a, r_k, ln_w, ln_b):
    B, S, _ = p.shape
    f32 = jnp.float32
    p = p + (token_shift(p) - p) * mu
    o1, o2, o3 = W_B, 2 * W_B, 3 * W_B
    r, k, v = p[..., :o1], p[..., o1:o2], p[..., o2:o3]
    xw = p[..., o3:o3 + R_W]
    xa = p[..., o3 + R_W:o3 + R_W + R_A]
    xg = p[..., o3 + R_W + R_A:]
    w = -jax.nn.softplus(-(w0 + jnp.tanh(xw) @ w2)) - 0.5
    decay = jnp.exp(-jnp.exp(w.astype(f32)))
    a = jax.nn.sigmoid(a0 + xa @ a2)
    g = jax.nn.sigmoid(xg) @ g2
    heads = lambda t: t.astype(f32).reshape(B, S, H_B, N_B)
    kk = heads(k * k_k)
    kk = kk / jnp.maximum(jnp.sqrt(jnp.sum(kk * kk, axis=-1, keepdims=True)), 1e-12)
    k = k * (1.0 + (a - 1.0) * k_a)
    rh, kh, vh, ah = heads(r), heads(k), heads(v), heads(a)
    o = rwkv7_scan(rh, heads(decay), kh, vh, kk, ah)
    mean = jnp.mean(o, axis=-1, keepdims=True)
    var = jnp.mean(jnp.square(o - mean), axis=-1, keepdims=True)
    o = ((o - mean) * lax.rsqrt(var + RWKV_LN_EPS)).reshape(B, S, W_B)
    o = o * ln_w.astype(f32) + ln_b.astype(f32)
    bonus = jnp.sum(rh * kh * r_k.astype(f32), axis=-1, keepdims=True) * vh
    o = o + bonus.reshape(B, S, W_B)
    return (o * g.astype(f32)).astype(p.dtype)


def hgrn2_chunked(q, k, v, log_f):
    B, S, H, dk = q.shape
    dv = v.shape[-1]
    C = HGRN_CHUNK
    n = S // C
    to_chunks = lambda t: t.reshape(B, n, C, H, t.shape[-1]).transpose(1, 0, 3, 2, 4)
    causal = jnp.tril(jnp.ones((C, C), dtype=bool))

    def step(state, inp):
        qc, kc, vc, lfc = inp
        b = jnp.cumsum(lfc, axis=2)
        o_inter = jnp.einsum('bhtk,bhkv->bhtv', qc * jnp.exp(b), state)
        diff = jnp.where(causal[:, :, None], b[:, :, :, None, :] - b[:, :, None, :, :], -jnp.inf)
        scores = jnp.einsum('bhtk,bhsk,bhtsk->bhts', qc, kc, jnp.exp(diff))
        o_intra = jnp.einsum('bhts,bhsv->bhtv', scores, vc)
        b_end = b[:, :, -1:, :]
        state = (jnp.exp(b_end[:, :, 0, :])[..., None] * state
                 + jnp.einsum('bhsk,bhsv->bhkv', kc * jnp.exp(b_end - b), vc))
        return state, o_inter + o_intra

    xs = tuple(to_chunks(t) for t in (q, k, v, log_f))
    _, o = lax.scan(step, jnp.zeros((B, H, dk, dv), jnp.float32), xs)
    return o.transpose(1, 0, 3, 2, 4).reshape(B, S, H, dv)


def hgrn2_mix(p, lower_bound, norm_w):
    B, S, _ = p.shape
    f32 = jnp.float32
    q, fz, i, g = jnp.split(p, 4, axis=-1)
    q = jax.nn.silu(q.astype(f32)).reshape(B, S, H_C, DK_C)
    lb = lower_bound.astype(f32)
    log_f = jnp.logaddexp(jnp.log(lb), jnp.log1p(-lb) + jax.nn.log_sigmoid(fz.astype(f32)))
    log_f = log_f.reshape(B, S, H_C, DK_C)
    k = -jnp.expm1(log_f)
    o = hgrn2_chunked(q, k, i.astype(f32).reshape(B, S, H_C, DV_C), log_f)
    o = rms_norm(o, norm_w) * jax.nn.silu(g.astype(f32).reshape(B, S, H_C, DV_C))
    return o.reshape(B, S, W_C).astype(p.dtype)


def hybrid_mixer(u, positions, layer_idx, w_in, qk_norm_w, lambda_qk, subln_w, w_out_a,
                 rwkv_mu, rwkv_w0, rwkv_w2, rwkv_a0, rwkv_a2, rwkv_g2, rwkv_k_k, rwkv_k_a,
                 rwkv_r_k, rwkv_ln_w, rwkv_ln_b, w_out_b, lower_bound, hgrn_norm_w, w_out_c, w_out):
    B, S, _ = u.shape
    p = u @ w_in
    pa = p[..., :P_A]
    pb = p[..., P_A:P_A + P_B]
    pc = p[..., P_A + P_B:P_A + P_B + P_C]
    pg = p[..., P_A + P_B + P_C:]
    qa = pa[..., :W_A].reshape(B, S, H_A, 2, DH_A)
    ka = pa[..., W_A:2 * W_A].reshape(B, S, H_A, 2, DH_A)
    va = pa[..., 2 * W_A:].reshape(B, S, H_A, 2 * DH_A)
    y_a = diff_attention(qa, ka, va, positions, qk_norm_w, lambda_qk, subln_w, layer_idx) @ w_out_a
    y_b = rwkv7_time_mix(pb, rwkv_mu, rwkv_w0, rwkv_w2, rwkv_a0, rwkv_a2, rwkv_g2,
                         rwkv_k_k, rwkv_k_a, rwkv_r_k, rwkv_ln_w, rwkv_ln_b) @ w_out_b
    y_c = hgrn2_mix(pc, lower_bound, hgrn_norm_w) @ w_out_c
    g_a, g_b, g_c = jnp.split(jax.nn.sigmoid(pg), 3, axis=-1)
    return (g_a * y_a + g_b * y_b + g_c * y_c) @ w_out


def setup_inputs(seed: int = 0) -> dict:
    key = jax.random.key(seed)
    ks = iter(jax.random.split(key, 40))
    f32 = jnp.float32
    L, D = DEPTH, D_MODEL

    def nrm(shape, scale):
        return jax.random.normal(next(ks), shape, f32) * scale

    def gain(shape):
        return 1.0 + nrm(shape, 0.02)

    x = nrm((BATCH, SEQ, D), 1.0)
    c = nrm((BATCH, D), 1.0)
    offsets = jax.random.randint(next(ks), (BATCH, 1), 0, 4096, dtype=jnp.int32)
    positions = (offsets + jnp.arange(SEQ, dtype=jnp.int32)[None, :]).astype(jnp.int32)
    return {
        'x': x,
        'c': c,
        'positions': positions,
        'mod_w': nrm((L, D, N_MOD * D), 0.5 * D ** -0.5),
        'mod_b': nrm((L, N_MOD * D), 0.02),
        'norm_w': gain((L, 3, D)),
        'ffn_w_gate': nrm((L, 2, D, D_FF), D ** -0.5),
        'ffn_w_up': nrm((L, 2, D, D_FF), D ** -0.5),
        'ffn_w_down': nrm((L, 2, D_FF, D), D_FF ** -0.5),
        'w_in': nrm((L, D, P_TOTAL), D ** -0.5),
        'qk_norm_w': gain((L, 2, DH_A)),
        'lambda_qk': nrm((L, 4, DH_A), 0.1),
        'subln_w': gain((L, 2 * DH_A)),
        'w_out_a': nrm((L, W_A, D), W_A ** -0.5),
        'rwkv_mu': jax.random.uniform(next(ks), (L, P_B), f32, 0.0, 1.0),
        'rwkv_w0': jax.random.uniform(next(ks), (L, W_B), f32, -6.0, 0.0),
        'rwkv_w2': nrm((L, R_W, W_B), R_W ** -0.5),
        'rwkv_a0': nrm((L, W_B), 0.1),
        'rwkv_a2': nrm((L, R_A, W_B), R_A ** -0.5),
        'rwkv_g2': nrm((L, R_G, W_B), R_G ** -0.5),
        'rwkv_k_k': 0.85 + nrm((L, W_B), 0.05),
        'rwkv_k_a': 1.0 + nrm((L, W_B), 0.05),
        'rwkv_r_k': nrm((L, H_B, N_B), 0.1),
        'rwkv_ln_w': gain((L, W_B)),
        'rwkv_ln_b': nrm((L, W_B), 0.02),
        'w_out_b': nrm((L, W_B, D), W_B ** -0.5),
        'hgrn_lower_bounds': nrm((L, W_C), 0.5),
        'hgrn_norm_w': gain((L, DV_C)),
        'w_out_c': nrm((L, W_C, D), W_C ** -0.5),
        'w_out': nrm((L, D, D), D ** -0.5),
    }


def reference(x, c, positions, mod_w, mod_b, norm_w, ffn_w_gate, ffn_w_up, ffn_w_down,
              w_in, qk_norm_w, lambda_qk, subln_w, w_out_a,
              rwkv_mu, rwkv_w0, rwkv_w2, rwkv_a0, rwkv_a2, rwkv_g2, rwkv_k_k, rwkv_k_a,
              rwkv_r_k, rwkv_ln_w, rwkv_ln_b, w_out_b,
              hgrn_lower_bounds, hgrn_norm_w, w_out_c, w_out):
    B, S, D = x.shape
    lb = jnp.cumsum(jax.nn.softmax(hgrn_lower_bounds.astype(jnp.float32), axis=0), axis=0)
    lb = lb - lb[0]
    cond = jax.nn.silu(c)
    h = x
    for l in range(DEPTH):
        mod = (cond @ mod_w[l] + mod_b[l]).reshape(B, N_MOD, D)
        m = [mod[:, j, None, :] for j in range(N_MOD)]
        u = modulate(rms_norm(h, norm_w[l, 0]), m[0], m[1])
        h = h + 0.5 * m[2] * swiglu(u, ffn_w_gate[l, 0], ffn_w_up[l, 0], ffn_w_down[l, 0])
        u = modulate(rms_norm(h, norm_w[l, 1]), m[3], m[4])
        y = hybrid_mixer(u, positions, l, w_in[l], qk_norm_w[l], lambda_qk[l], subln_w[l], w_out_a[l],
                         rwkv_mu[l], rwkv_w0[l], rwkv_w2[l], rwkv_a0[l], rwkv_a2[l], rwkv_g2[l],
                         rwkv_k_k[l], rwkv_k_a[l], rwkv_r_k[l], rwkv_ln_w[l], rwkv_ln_b[l], w_out_b[l],
                         lb[l], hgrn_norm_w[l], w_out_c[l], w_out[l])
        h = h + m[5] * y
        u = modulate(rms_norm(h, norm_w[l, 2]), m[6], m[7])
        h = h + 0.5 * m[8] * swiglu(u, ffn_w_gate[l, 1], ffn_w_up[l, 1], ffn_w_down[l, 1])
    return h
```

```python
import functools
import math

import jax
import jax.numpy as jnp
from jax import lax
from jax.experimental import pallas as pl
from jax.experimental.pallas import tpu as pltpu

F32 = jnp.float32
BF16 = jnp.bfloat16

H_A, DH_A = 4, 64
W_A = H_A * 2 * DH_A
ROPE_THETA = 10000.0
H_B, N_B = 8, 64
W_B = H_B * N_B
R_W, R_A, R_G = 64, 64, 160
RWKV_LN_EPS = 64e-5
H_C, DK_C = 4, 128
W_C = H_C * DK_C
N_MOD = 9
EPS = 1e-6
P_A = 3 * W_A
P_B = 3 * W_B + R_W + R_A + R_G
P_C = 4 * W_C

LANES = 128
R_G_PAD = 256
P_B_PAD = 3 * W_B + R_W + R_A + R_G_PAD
VMEM_LIMIT = 56 * 1024 * 1024

FFN_TM = 512
MIX_TM = 256
ATT_T = 512
RWKV_C = 64
HGRN_C = 64
HGRN_SUB = 16
MERGE_TM = 512


def _cparams(n_axes):
    return pltpu.CompilerParams(dimension_semantics=("arbitrary",) * n_axes,
                                vmem_limit_bytes=VMEM_LIMIT)


def _const_spec(shape):
    nd = len(shape)
    return pl.BlockSpec(shape, lambda *_: (0,) * nd, pipeline_mode=pl.Buffered(1))


def _sigmoid(x):
    return 1.0 / (1.0 + jnp.exp(-x))


def _silu(x):
    return x * _sigmoid(x)


def _softplus(x):
    return jnp.maximum(x, 0.0) + jnp.log1p(jnp.exp(-jnp.abs(x)))


def _dot(a, b):
    return jnp.dot(a, b, preferred_element_type=F32)


def _dot_nt(a, b):
    return lax.dot_general(a, b, (((1,), (1,)), ((), ())), preferred_element_type=F32)


def _dot_tn(a, b):
    return lax.dot_general(a, b, (((0,), (0,)), ((), ())), preferred_element_type=F32)


def _group_sum(x, g_ref):
    hi = x.astype(BF16)
    lo = (x - hi.astype(F32)).astype(BF16)
    g = g_ref[...]
    return _dot(hi, g) + _dot(lo, g)


def _norm_mod(x, nw, shift, scale):
    ms = jnp.mean(x * x, axis=-1, keepdims=True)
    return (x * lax.rsqrt(ms + EPS) * nw) * (1.0 + scale) + shift


def _cumsum_rows(x, n):
    row = lax.broadcasted_iota(jnp.int32, (n, 1), 0)
    s = 1
    while s < n:
        x = x + jnp.where(row >= s, pltpu.roll(x, s, 0), 0.0)
        s *= 2
    return x


def _mod_kernel(c_ref, w_ref, b_ref, o_ref):
    c = c_ref[...]
    cond = _silu(c)
    o_ref[0] = jnp.dot(cond, w_ref[0], preferred_element_type=F32,
                       precision=lax.Precision.HIGHEST) + b_ref[0]


def _modulation(c, mod_w, mod_b):
    L, D, N = mod_w.shape
    B = c.shape[0]
    tn = 1024
    return pl.pallas_call(
        _mod_kernel,
        grid=(L, N // tn),
        in_specs=[pl.BlockSpec((B, D), lambda l, j: (0, 0)),
                  pl.BlockSpec((1, D, tn), lambda l, j: (l, 0, j)),
                  pl.BlockSpec((1, 1, tn), lambda l, j: (l, 0, j))],
        out_specs=pl.BlockSpec((1, B, tn), lambda l, j: (l, 0, j)),
        out_shape=jax.ShapeDtypeStruct((L, B, N), F32),
        compiler_params=_cparams(2),
        name="modulation",
    )(c, mod_w, mod_b.reshape(L, 1, N))


def _rope_kernel(pos_ref, inv_ref, sign_ref, cos_ref, sin_ref):
    ang = pos_ref[0] * inv_ref[...]
    cos_ref[0] = jnp.cos(ang)
    sin_ref[0] = jnp.sin(ang) * sign_ref[...]


def _rope_tables(positions):
    B, S = positions.shape
    ts = 512 if S % 512 == 0 else S
    half = DH_A // 2
    inv = ROPE_THETA ** (-jnp.arange(0, DH_A, 2, dtype=F32) / DH_A)
    inv128 = jnp.tile(inv, LANES // half)[None, :]
    sign128 = jnp.tile(jnp.concatenate([-jnp.ones(half, F32), jnp.ones(half, F32)]), LANES // DH_A)[None, :]
    pos = positions.astype(F32)[..., None]
    return pl.pallas_call(
        _rope_kernel,
        grid=(B, S // ts),
        in_specs=[pl.BlockSpec((1, ts, 1), lambda b, s: (b, s, 0)),
                  pl.BlockSpec((1, LANES), lambda b, s: (0, 0)),
                  pl.BlockSpec((1, LANES), lambda b, s: (0, 0))],
        out_specs=[pl.BlockSpec((1, ts, LANES), lambda b, s: (b, s, 0))] * 2,
        out_shape=[jax.ShapeDtypeStruct((B, S, LANES), F32)] * 2,
        compiler_params=_cparams(2),
        name="rope_tables",
    )(pos, inv128, sign128)


def _ffn_kernel(chunks, shift_i, h_ref, mod_ref, nw_ref, wg_ref, wu_ref, wd_ref, o_ref, act_ref):
    x = h_ref[0]
    m = mod_ref[0]
    u = _norm_mod(x, nw_ref[...], m[shift_i:shift_i + 1], m[shift_i + 1:shift_i + 2]).astype(BF16)
    for lo, hi in chunks:
        g = _dot(u, wg_ref[:, lo:hi])
        up = _dot(u, wu_ref[:, lo:hi])
        act_ref[:, lo:hi] = (_silu(g) * up).astype(BF16)
    y = _dot(act_ref[...], wd_ref[...])
    o_ref[0] = x + (0.5 * m[shift_i + 2:shift_i + 3]) * y


def _ffn(h, mod, nw, wg, wu, wd, shift_i):
    B, S, D = h.shape
    F = wg.shape[1]
    tm = min(FFN_TM, S)
    chunks, lo = [], 0
    while lo < F:
        hi = min(lo + 512, F)
        chunks.append((lo, hi))
        lo = hi
    return pl.pallas_call(
        functools.partial(_ffn_kernel, tuple(chunks), shift_i),
        grid=(B, S // tm),
        in_specs=[pl.BlockSpec((1, tm, D), lambda b, s: (b, s, 0)),
                  pl.BlockSpec((1, N_MOD, D), lambda b, s: (b, 0, 0)),
                  _const_spec((1, D)), _const_spec((D, F)), _const_spec((D, F)), _const_spec((F, D))],
        out_specs=pl.BlockSpec((1, tm, D), lambda b, s: (b, s, 0)),
        out_shape=jax.ShapeDtypeStruct((B, S, D), F32),
        scratch_shapes=[pltpu.VMEM((tm, F), BF16)],
        compiler_params=_cparams(2),
        name="ffn",
    )(h, mod, nw, wg, wu, wd)


_O_QA, _O_KA, _O_VA = 0, W_A, 2 * W_A
_O_B = P_A
_O_C = P_A + P_B_PAD
_O_G = _O_C + P_C


def _mixin_kernel(h_ref, mod_ref, nw_ref, w_ref, cos_ref, sin_ref, qkw_ref, g64_ref, mu_ref,
                  w0_ref, w2_ref, a0_ref, a2_ref, g2_ref, kk_ref, ka_ref, rk_ref, llb_ref, l1lb_ref,
                  qa_o, ka_o, va_o, r_o, k_o, v_o, kkn_o, beta_o, logw_o, g_o, bonus_o,
                  qc_o, logf_o, kc_o, ic_o, gc_o, gates_o, prev_ref):
    tm = h_ref.shape[1]
    D = h_ref.shape[2]
    x = h_ref[0]
    m = mod_ref[0]
    u = _norm_mod(x, nw_ref[...], m[3:4], m[4:5]).astype(BF16)

    def proj(lo, hi):
        return _dot(u, w_ref[:, lo:hi])

    cos4 = jnp.concatenate([cos_ref[0]] * (W_A // LANES), axis=1)
    sin4 = jnp.concatenate([sin_ref[0]] * (W_A // LANES), axis=1)
    lane = lax.broadcasted_iota(jnp.int32, (1, W_A), 1)
    first_half = (lane % DH_A) < (DH_A // 2)

    def qk_prep(p, wrow, scale):
        ss = _group_sum(p * p, g64_ref)
        n = p * lax.rsqrt(ss * (1.0 / DH_A) + EPS) * wrow
        rot = jnp.where(first_half, pltpu.roll(n, W_A - DH_A // 2, 1), pltpu.roll(n, DH_A // 2, 1))
        return ((n * cos4 + rot * sin4) * scale).astype(BF16)

    qkw = qkw_ref[...]
    qa_o[0] = qk_prep(proj(_O_QA, _O_QA + W_A), qkw[0:1], DH_A ** -0.5)
    ka_o[0] = qk_prep(proj(_O_KA, _O_KA + W_A), qkw[1:2], 1.0)
    va_o[0] = proj(_O_VA, _O_VA + W_A).astype(BF16)

    pb = proj(_O_B, _O_B + P_B_PAD)

    @pl.when(pl.program_id(1) == 0)
    def _():
        prev_ref[...] = jnp.zeros_like(prev_ref)

    row = lax.broadcasted_iota(jnp.int32, (tm, 1), 0)
    shifted = jnp.where(row == 0, prev_ref[...], pltpu.roll(pb, 1, 0))
    prev_ref[...] = pb[tm - 1:tm]
    pb = pb + (shifted - pb) * mu_ref[...]
    r = pb[:, 0:W_B]
    k = pb[:, W_B:2 * W_B]
    v = pb[:, 2 * W_B:3 * W_B]
    xwa = pb[:, 3 * W_B:3 * W_B + R_W + R_A]
    xg = pb[:, 3 * W_B + R_W + R_A:]
    w = -_softplus(-(w0_ref[...] + _dot(jnp.tanh(xwa).astype(BF16), w2_ref[...]))) - 0.5
    logw_o[0] = -jnp.exp(w)
    a = _sigmoid(a0_ref[...] + _dot(xwa.astype(BF16), a2_ref[...]))
    g_o[0] = _dot(_sigmoid(xg).astype(BF16), g2_ref[...]).astype(BF16)
    kk = k * kk_ref[...]
    kkn = kk / jnp.maximum(jnp.sqrt(_group_sum(kk * kk, g64_ref)), 1e-12)
    k2 = k * (1.0 + (a - 1.0) * ka_ref[...])
    r_o[0] = r.astype(BF16)
    k_o[0] = k2.astype(BF16)
    v_o[0] = v.astype(BF16)
    kkn_o[0] = kkn.astype(BF16)
    beta_o[0] = (kkn * a).astype(BF16)
    bonus_o[0] = (_group_sum(r * k2 * rk_ref[...], g64_ref) * v).astype(BF16)

    qc_o[0] = _silu(proj(_O_C, _O_C + W_C)).astype(BF16)
    fz = proj(_O_C + W_C, _O_C + 2 * W_C)
    log_sig = jnp.minimum(fz, 0.0) - jnp.log1p(jnp.exp(-jnp.abs(fz)))
    t1 = llb_ref[...]
    t2 = l1lb_ref[...] + log_sig
    logf = jnp.maximum(t1, t2) + jnp.log1p(jnp.exp(-jnp.abs(t1 - t2)))
    logf_o[0] = logf
    kc_o[0] = (1.0 - jnp.exp(logf)).astype(BF16)
    ic_o[0] = proj(_O_C + 2 * W_C, _O_C + 3 * W_C).astype(BF16)
    gc_o[0] = _silu(proj(_O_C + 3 * W_C, _O_C + 4 * W_C)).astype(BF16)

    gates_o[0] = _sigmoid(proj(_O_G, _O_G + 3 * D)).astype(BF16)


def _mixer_in(h, mod, nw, w_pad, cos, sin, qkw, g64, mu, w0, w2p, a0, a2p, g2p, k_k, k_a, r_k, llb, l1lb):
    B, S, D = h.shape
    tm = min(MIX_TM, S)
    PT = w_pad.shape[1]
    tok = lambda width: pl.BlockSpec((1, tm, width), lambda b, s: (b, s, 0))
    out_widths = [W_A] * 3 + [W_B] * 8 + [W_C] * 5 + [3 * D]
    out_dtypes = [BF16] * 3 + [BF16] * 5 + [F32, BF16, BF16] + [BF16, F32, BF16, BF16, BF16] + [BF16]
    return pl.pallas_call(
        _mixin_kernel,
        grid=(B, S // tm),
        in_specs=[tok(D), pl.BlockSpec((1, N_MOD, D), lambda b, s: (b, 0, 0)), _const_spec((1, D)),
                  _const_spec((D, PT)), tok(LANES), tok(LANES), _const_spec((2, W_A)), _const_spec((W_B, W_B)),
                  _const_spec((1, P_B_PAD)), _const_spec((1, W_B)), _const_spec((R_W + R_A, W_B)),
                  _const_spec((1, W_B)), _const_spec((R_W + R_A, W_B)), _const_spec((R_G_PAD, W_B)),
                  _const_spec((1, W_B)), _const_spec((1, W_B)), _const_spec((1, W_B)),
                  _const_spec((1, W_C)), _const_spec((1, W_C))],
        out_specs=[tok(wd) for wd in out_widths],
        out_shape=[jax.ShapeDtypeStruct((B, S, wd), dt) for wd, dt in zip(out_widths, out_dtypes)],
        scratch_shapes=[pltpu.VMEM((1, P_B_PAD), F32)],
        compiler_params=_cparams(2),
        name="mixer_in",
    )(h, mod, nw, w_pad, cos, sin, qkw, g64, mu, w0, w2p, a0, a2p, g2p, k_k, k_a, r_k, llb, l1lb)


def _attn_kernel(lam_init, q_ref, k_ref, v_ref, lq_ref, sw_ref, o_ref, qs_ref, m_ref, l_ref, acc_ref):
    t = q_ref.shape[1]
    qi = pl.program_id(2)
    ki = pl.program_id(3)

    @pl.when(ki == 0)
    def _():
        q = q_ref[0]
        lane = lax.broadcasted_iota(jnp.int32, (1, LANES), 1)
        zero = jnp.zeros_like(q)
        qs_ref[0:t] = jnp.where(lane < DH_A, q, zero)
        qs_ref[t:2 * t] = jnp.where(lane < DH_A, zero, q)
        m_ref[...] = jnp.full_like(m_ref, -jnp.inf)
        l_ref[...] = jnp.zeros_like(l_ref)
        acc_ref[...] = jnp.zeros_like(acc_ref)

    def step(masked):
        s = _dot_nt(qs_ref[...], k_ref[0])
        if masked:
            r_i = lax.broadcasted_iota(jnp.int32, (2 * t, t), 0) % t
            c_i = lax.broadcasted_iota(jnp.int32, (2 * t, t), 1)
            s = jnp.where(c_i <= r_i, s, -jnp.inf)
        m_old = m_ref[...]
        m_new = jnp.maximum(m_old, jnp.max(s, axis=-1, keepdims=True))
        alpha = jnp.exp(m_old - m_new)
        p = jnp.exp(s - m_new)
        l_ref[...] = alpha * l_ref[...] + jnp.sum(p, axis=-1, keepdims=True)
        acc_ref[...] = alpha * acc_ref[...] + _dot(p.astype(BF16), v_ref[0])
        m_ref[...] = m_new

    @pl.when(ki < qi)
    def _():
        step(False)

    @pl.when(ki == qi)
    def _():
        step(True)
        lq = lq_ref[...]
        lam = (jnp.exp(jnp.sum(lq[0:1] * lq[1:2], axis=-1, keepdims=True))
               - jnp.exp(jnp.sum(lq[2:3] * lq[3:4], axis=-1, keepdims=True)) + lam_init)
        o = acc_ref[...] / l_ref[...]
        o = o[0:t] - lam * o[t:2 * t]
        ms = jnp.mean(o * o, axis=-1, keepdims=True)
        o_ref[0] = (o * lax.rsqrt(ms + EPS) * sw_ref[...] * (1.0 - lam_init)).astype(BF16)


def _diff_attention(qa, ka, va, lambda_qk, subln_w, layer_idx):
    B, S, _ = qa.shape
    t = min(ATT_T, S)
    n = S // t
    lam_init = 0.8 - 0.6 * math.exp(-0.3 * layer_idx)
    q_spec = pl.BlockSpec((1, t, LANES), lambda b, h, i, j: (b, i, h))
    kv_spec = pl.BlockSpec((1, t, LANES), lambda b, h, i, j: (b, jnp.minimum(i, j), h))
    return pl.pallas_call(
        functools.partial(_attn_kernel, lam_init),
        grid=(B, H_A, n, n),
        in_specs=[q_spec, kv_spec, kv_spec,
                  pl.BlockSpec((4, DH_A), lambda b, h, i, j: (0, 0)),
                  pl.BlockSpec((1, 2 * DH_A), lambda b, h, i, j: (0, 0))],
        out_specs=pl.BlockSpec((1, t, LANES), lambda b, h, i, j: (b, i, h)),
        out_shape=jax.ShapeDtypeStruct((B, S, W_A), BF16),
        scratch_shapes=[pltpu.VMEM((2 * t, LANES), BF16), pltpu.VMEM((2 * t, 1), F32),
                        pltpu.VMEM((2 * t, 1), F32), pltpu.VMEM((2 * t, LANES), F32)],
        compiler_params=_cparams(4),
        name="diff_attention",
    )(qa, ka, va, lambda_qk, subln_w)


def _rwkv_kernel(r_ref, k_ref, v_ref, kk_ref, beta_ref, lw_ref, g_ref, bonus_ref, lnw_ref, lnb_ref, g64_ref,
                 o_ref, state_ref):
    C = r_ref.shape[1]
    C2 = 2 * C

    @pl.when(pl.program_id(1) == 0)
    def _():
        state_ref[...] = jnp.zeros_like(state_ref)

    lw = lw_ref[0]
    cum = _cumsum_rows(lw, C)
    c_last = cum[C - 1:C]
    p_incl = jnp.exp(cum)
    p_prev = jnp.exp(cum - lw)
    p_inv = jnp.exp(-cum)
    p_end = jnp.exp(c_last - cum)
    p_last = jnp.exp(c_last)
    r = r_ref[0].astype(F32)
    k = k_ref[0].astype(F32)
    kk = kk_ref[0].astype(F32)
    beta = beta_ref[0].astype(F32)
    v_all = v_ref[0]
    rt = (r * p_incl).astype(BF16)
    kt = (kk * p_prev).astype(BF16)
    kh = (k * p_inv).astype(BF16)
    bh = (beta * p_inv).astype(BF16)
    ke = (k * p_end).astype(BF16)
    be = (beta * p_end).astype(BF16)

    lane = lax.broadcasted_iota(jnp.int32, (1, LANES), 1)
    head0 = lane < N_B
    ri = lax.broadcasted_iota(jnp.int32, (C2, 1), 0)
    ci = lax.broadcasted_iota(jnp.int32, (1, C2), 1)
    same = (ri < C) == (ci < C)
    strict = same & ((ci % C) < (ri % C))
    incl = same & ((ci % C) <= (ri % C))
    eye = jnp.where(ri == ci, 1.0, 0.0)
    same_state = (lax.broadcasted_iota(jnp.int32, (LANES, 1), 0) < N_B) == head0

    def stack(x):
        z = jnp.zeros_like(x)
        return jnp.concatenate([jnp.where(head0, x, z), jnp.where(head0, z, x)], axis=0)

    def fold(x):
        return x[0:C] + x[C:C2]

    outs = []
    for p in range(H_B // 2):
        sl = slice(p * LANES, (p + 1) * LANES)
        v = v_all[:, sl]
        lhs = jnp.concatenate([stack(kt[:, sl]), stack(rt[:, sl])], axis=0)
        rk_ = _dot_nt(lhs, jnp.concatenate([kh[:, sl]] * 2, axis=0))
        rb_ = _dot_nt(lhs, jnp.concatenate([bh[:, sl]] * 2, axis=0))
        a_kk = jnp.where(strict, rk_[0:C2], 0.0)
        a_kb = jnp.where(strict, rb_[0:C2], 0.0)
        a_rk = jnp.where(incl, rk_[C2:], 0.0)
        a_rb = jnp.where(incl, rb_[C2:], 0.0)
        xm = -a_kb
        pw = xm.astype(BF16)
        s = 2
        while s < C:
            pw2 = _dot(pw, pw)
            pw = pw2.astype(BF16)
            xm = xm + pw2 + _dot(xm.astype(BF16), pw)
            s *= 2
        st = state_ref[p]
        w1r1 = _dot_nt(jnp.concatenate([kt[:, sl], rt[:, sl]], axis=0), st.astype(BF16))
        v_s = stack(v)
        rhs = w1r1[0:C] + fold(_dot(a_kk.astype(BF16), v_s))
        u = rhs + fold(_dot(xm.astype(BF16), stack(rhs.astype(BF16))))
        u_b = u.astype(BF16)
        o = w1r1[C:C2] + fold(_dot(jnp.concatenate([a_rk, -a_rb], axis=1).astype(BF16),
                                   jnp.concatenate([v_s, stack(u_b)], axis=0)))
        upd = _dot_tn(jnp.concatenate([v, u_b], axis=0), jnp.concatenate([ke[:, sl], -be[:, sl]], axis=0))
        state_ref[p] = st * p_last[:, sl] + jnp.where(same_state, upd, 0.0)
        outs.append(o)

    o = jnp.concatenate(outs, axis=1)
    mean = _group_sum(o, g64_ref) * (1.0 / N_B)
    d = o - mean
    var = _group_sum(d * d, g64_ref) * (1.0 / N_B)
    y = d * lax.rsqrt(var + RWKV_LN_EPS) * lnw_ref[...] + lnb_ref[...] + bonus_ref[0].astype(F32)
    o_ref[0] = (y * g_ref[0].astype(F32)).astype(BF16)


def _rwkv_scan(r, k, v, kkn, beta, logw, g, bonus, ln_w, ln_b, g64):
    B, S, W = r.shape
    C = RWKV_C
    tok = pl.BlockSpec((1, C, W), lambda b, c: (b, c, 0))
    return pl.pallas_call(
        _rwkv_kernel,
        grid=(B, S // C),
        in_specs=[tok] * 8 + [_const_spec((1, W)), _const_spec((1, W)), _const_spec((W, W))],
        out_specs=tok,
        out_shape=jax.ShapeDtypeStruct((B, S, W), BF16),
        scratch_shapes=[pltpu.VMEM((H_B // 2, LANES, LANES), F32)],
        compiler_params=_cparams(2),
        name="rwkv7_scan",
    )(r, k, v, kkn, beta, logw, g, bonus, ln_w, ln_b, g64)


def _hgrn_kernel(q_ref, k_ref, v_ref, lf_ref, gate_ref, nw_ref, o_ref, state_ref):
    C = q_ref.shape[1]
    c = HGRN_SUB

    @pl.when(pl.program_id(1) == 0)
    def _():
        state_ref[...] = jnp.zeros_like(state_ref)

    b = _cumsum_rows(lf_ref[0], C)
    b_end = b[C - 1:C]
    q = q_ref[0].astype(F32)
    k = k_ref[0].astype(F32)
    v_all = v_ref[0]
    q_in = (q * jnp.exp(b)).astype(BF16)
    k_end = (k * jnp.exp(b_end - b)).astype(BF16)
    p_end = jnp.exp(b_end)
    lane_c = lax.broadcasted_iota(jnp.int32, (1, c), 1)
    row_c = lax.broadcasted_iota(jnp.int32, (c, 1), 0)

    outs = []
    for h in range(H_C):
        sl = slice(h * LANES, (h + 1) * LANES)
        st = state_ref[h]
        o_inter = _dot_nt(q_in[:, sl], st.astype(BF16))
        v = v_all[:, sl]
        blocks = []
        for i in range(C // c):
            rows = slice(i * c, (i + 1) * c)
            b_i = b[rows, sl]
            q_i = q[rows, sl]
            k_i = k[rows, sl]
            sc = jnp.zeros((c, c), F32)
            for j in range(c):
                e = jnp.exp(jnp.minimum(b_i - b_i[j:j + 1], 0.0))
                col = jnp.sum(q_i * e * k_i[j:j + 1], axis=-1, keepdims=True)
                sc = jnp.where(lane_c == j, col, sc)
            sc = jnp.where(lane_c <= row_c, sc, 0.0)
            o_i = _dot(sc.astype(BF16), v[rows])
            if i > 0:
                b_ref_ = b[i * c - 1:i * c, sl]
                qs = (q_i * jnp.exp(b_i - b_ref_)).astype(BF16)
                ks = (k[0:i * c, sl] * jnp.exp(b_ref_ - b[0:i * c, sl])).astype(BF16)
                o_i = o_i + _dot(_dot_nt(qs, ks).astype(BF16), v[0:i * c])
            blocks.append(o_i)
        o_h = o_inter + jnp.concatenate(blocks, axis=0)
        state_ref[h] = st * p_end[:, sl] + _dot_tn(v, k_end[:, sl])
        ms = jnp.mean(o_h * o_h, axis=-1, keepdims=True)
        outs.append(o_h * lax.rsqrt(ms + EPS) * nw_ref[...])
    o = jnp.concatenate(outs, axis=1)
    o_ref[0] = (o * gate_ref[0].astype(F32)).astype(BF16)


def _hgrn_scan(q, k, v, logf, gate, norm_w):
    B, S, W = q.shape
    C = min(HGRN_C, S)
    tok = pl.BlockSpec((1, C, W), lambda b, c: (b, c, 0))
    return pl.pallas_call(
        _hgrn_kernel,
        grid=(B, S // C),
        in_specs=[tok] * 5 + [_const_spec((1, LANES))],
        out_specs=tok,
        out_shape=jax.ShapeDtypeStruct((B, S, W), BF16),
        scratch_shapes=[pltpu.VMEM((H_C, LANES, LANES), F32)],
        compiler_params=_cparams(2),
        name="hgrn2_scan",
    )(q, k, v, logf, gate, norm_w)


def _merge_kernel(h_ref, mod_ref, oa_ref, ob_ref, oc_ref, gates_ref, wa_ref, wb_ref, wc_ref, wo_ref, o_ref):
    D = h_ref.shape[2]
    gates = gates_ref[0]
    z = (gates[:, 0:D].astype(F32) * _dot(oa_ref[0], wa_ref[...])
         + gates[:, D:2 * D].astype(F32) * _dot(ob_ref[0], wb_ref[...])
         + gates[:, 2 * D:3 * D].astype(F32) * _dot(oc_ref[0], wc_ref[...]))
    y = _dot(z.astype(BF16), wo_ref[...])
    o_ref[0] = h_ref[0] + mod_ref[0][5:6] * y


def _merge(h, mod, oa, ob, oc, gates, wa, wb, wc, wo):
    B, S, D = h.shape
    tm = min(MERGE_TM, S)
    tok = lambda width: pl.BlockSpec((1, tm, width), lambda b, s: (b, s, 0))
    return pl.pallas_call(
        _merge_kernel,
        grid=(B, S // tm),
        in_specs=[tok(D), pl.BlockSpec((1, N_MOD, D), lambda b, s: (b, 0, 0)),
                  tok(W_A), tok(W_B), tok(W_C), tok(3 * D),
                  _const_spec((W_A, D)), _const_spec((W_B, D)), _const_spec((W_C, D)), _const_spec((D, D))],
        out_specs=tok(D),
        out_shape=jax.ShapeDtypeStruct((B, S, D), F32),
        compiler_params=_cparams(2),
        name="merge",
    )(h, mod, oa, ob, oc, gates, wa, wb, wc, wo)


def _block_ones(width, group):
    i = jnp.arange(width) // group
    return (i[:, None] == i[None, :]).astype(BF16)


def kernel(x, c, positions, mod_w, mod_b, norm_w, ffn_w_gate, ffn_w_up, ffn_w_down, w_in, qk_norm_w, lambda_qk,
           subln_w, w_out_a, rwkv_mu, rwkv_w0, rwkv_w2, rwkv_a0, rwkv_a2, rwkv_g2, rwkv_k_k, rwkv_k_a, rwkv_r_k,
           rwkv_ln_w, rwkv_ln_b, w_out_b, hgrn_lower_bounds, hgrn_norm_w, w_out_c, w_out):
    B, S, D = x.shape
    L = mod_w.shape[0]
    lb = jnp.cumsum(jax.nn.softmax(hgrn_lower_bounds.astype(F32), axis=0), axis=0)
    lb = lb - lb[0]
    log_lb = jnp.log(lb)
    log1m_lb = jnp.log1p(-lb)

    mod = _modulation(c, mod_w, mod_b).reshape(L, B, N_MOD, D)
    cos, sin = _rope_tables(positions)
    g64 = _block_ones(W_B, N_B)
    pad_cols = R_G_PAD - R_G
    zrow = lambda n: jnp.zeros((n, W_B), F32)

    h = x
    for l in range(L):
        wg = ffn_w_gate[l].astype(BF16)
        wu = ffn_w_up[l].astype(BF16)
        wd = ffn_w_down[l].astype(BF16)
        h = _ffn(h, mod[l], norm_w[l, 0][None], wg[0], wu[0], wd[0], 0)

        w_pad = jnp.concatenate([w_in[l][:, :P_A + P_B], jnp.zeros((D, pad_cols), F32),
                                 w_in[l][:, P_A + P_B:]], axis=1).astype(BF16)
        mu = jnp.concatenate([rwkv_mu[l], jnp.zeros((pad_cols,), F32)])[None]
        w2p = jnp.concatenate([rwkv_w2[l], zrow(R_A)], axis=0).astype(BF16)
        a2p = jnp.concatenate([zrow(R_W), rwkv_a2[l]], axis=0).astype(BF16)
        g2p = jnp.concatenate([rwkv_g2[l], zrow(pad_cols)], axis=0).astype(BF16)
        qkw = jnp.tile(qk_norm_w[l], (1, W_A // DH_A))
        (qa, ka, va, r, k, v, kkn, beta, logw, g, bonus, qc, logf, kc, ic, gc, gates) = _mixer_in(
            h, mod[l], norm_w[l, 1][None], w_pad, cos, sin, qkw, g64, mu,
            rwkv_w0[l][None], w2p, rwkv_a0[l][None], a2p, g2p, rwkv_k_k[l][None], rwkv_k_a[l][None],
            rwkv_r_k[l].reshape(1, W_B), log_lb[l][None], log1m_lb[l][None])
        oa = _diff_attention(qa, ka, va, lambda_qk[l], subln_w[l][None], l)
        ob = _rwkv_scan(r, k, v, kkn, beta, logw, g, bonus, rwkv_ln_w[l][None], rwkv_ln_b[l][None], g64)
        oc = _hgrn_scan(qc, kc, ic, logf, gc, hgrn_norm_w[l][None])
        h = _merge(h, mod[l], oa, ob, oc, gates, w_out_a[l].astype(BF16), w_out_b[l].astype(BF16),
                   w_out_c[l].astype(BF16), w_out[l].astype(BF16))

        h = _ffn(h, mod[l], norm_w[l, 2][None], wg[1], wu[1], wd[1], 6)
    return h
```

```python
import functools
import math

import jax
import jax.numpy as jnp
from jax import lax
from jax.experimental import pallas as pl
from jax.experimental.pallas import tpu as pltpu

F32 = jnp.float32
BF16 = jnp.bfloat16

H_A, DH_A = 4, 64
W_A = H_A * 2 * DH_A
ROPE_THETA = 10000.0
H_B, N_B = 8, 64
W_B = H_B * N_B
R_W, R_A, R_G = 64, 64, 160
RWKV_LN_EPS = 64e-5
H_C, DK_C = 4, 128
W_C = H_C * DK_C
N_MOD = 9
EPS = 1e-6
P_A = 3 * W_A
P_B = 3 * W_B + R_W + R_A + R_G
P_C = 4 * W_C

LANES = 128
R_G_PAD = 256
P_B_PAD = 3 * W_B + R_W + R_A + R_G_PAD
VMEM_LIMIT = 56 * 1024 * 1024

FFN_TM = 512
MIX_TM = 256
ATT_T = 512
RWKV_C = 64
RWKV_T = 256
HGRN_C = 64
HGRN_SUB = 16
MERGE_TM = 512


def _cparams(n_axes):
    return pltpu.CompilerParams(dimension_semantics=("arbitrary",) * n_axes,
                                vmem_limit_bytes=VMEM_LIMIT)


def _const_spec(shape):
    nd = len(shape)
    return pl.BlockSpec(shape, lambda *_: (0,) * nd, pipeline_mode=pl.Buffered(1))


def _sigmoid(x):
    return 1.0 / (1.0 + jnp.exp(-x))


def _silu(x):
    return x * _sigmoid(x)


def _softplus(x):
    return jnp.maximum(x, 0.0) + jnp.log1p(jnp.exp(-jnp.abs(x)))


def _dot(a, b):
    return jnp.dot(a, b, preferred_element_type=F32)


def _dot_nt(a, b):
    return lax.dot_general(a, b, (((1,), (1,)), ((), ())), preferred_element_type=F32)


def _dot_tn(a, b):
    return lax.dot_general(a, b, (((0,), (0,)), ((), ())), preferred_element_type=F32)


def _group_sum(x, g_ref):
    hi = x.astype(BF16)
    lo = (x - hi.astype(F32)).astype(BF16)
    g = g_ref[...]
    return _dot(hi, g) + _dot(lo, g)


def _norm_mod(x, nw, shift, scale):
    ms = jnp.mean(x * x, axis=-1, keepdims=True)
    return (x * lax.rsqrt(ms + EPS) * nw) * (1.0 + scale) + shift


def _cumsum_rows(x, n):
    row = lax.broadcasted_iota(jnp.int32, (n, 1), 0)
    s = 1
    while s < n:
        x = x + jnp.where(row >= s, pltpu.roll(x, s, 0), 0.0)
        s *= 2
    return x


def _mod_kernel(c_ref, w_ref, b_ref, o_ref):
    c = c_ref[...]
    cond = _silu(c)
    o_ref[0] = jnp.dot(cond, w_ref[0], preferred_element_type=F32,
                       precision=lax.Precision.HIGHEST) + b_ref[0]


def _modulation(c, mod_w, mod_b):
    L, D, N = mod_w.shape
    B = c.shape[0]
    tn = 1024
    return pl.pallas_call(
        _mod_kernel,
        grid=(L, N // tn),
        in_specs=[pl.BlockSpec((B, D), lambda l, j: (0, 0)),
                  pl.BlockSpec((1, D, tn), lambda l, j: (l, 0, j)),
                  pl.BlockSpec((1, 1, tn), lambda l, j: (l, 0, j))],
        out_specs=pl.BlockSpec((1, B, tn), lambda l, j: (l, 0, j)),
        out_shape=jax.ShapeDtypeStruct((L, B, N), F32),
        compiler_params=_cparams(2),
        name="modulation",
    )(c, mod_w, mod_b.reshape(L, 1, N))


def _rope_kernel(pos_ref, inv_ref, sign_ref, cos_ref, sin_ref):
    ang = pos_ref[0] * inv_ref[...]
    cos_ref[0] = jnp.cos(ang)
    sin_ref[0] = jnp.sin(ang) * sign_ref[...]


def _rope_tables(positions):
    B, S = positions.shape
    ts = 512 if S % 512 == 0 else S
    half = DH_A // 2
    inv = ROPE_THETA ** (-jnp.arange(0, DH_A, 2, dtype=F32) / DH_A)
    inv128 = jnp.tile(inv, LANES // half)[None, :]
    sign128 = jnp.tile(jnp.concatenate([-jnp.ones(half, F32), jnp.ones(half, F32)]), LANES // DH_A)[None, :]
    pos = positions.astype(F32)[..., None]
    return pl.pallas_call(
        _rope_kernel,
        grid=(B, S // ts),
        in_specs=[pl.BlockSpec((1, ts, 1), lambda b, s: (b, s, 0)),
                  pl.BlockSpec((1, LANES), lambda b, s: (0, 0)),
                  pl.BlockSpec((1, LANES), lambda b, s: (0, 0))],
        out_specs=[pl.BlockSpec((1, ts, LANES), lambda b, s: (b, s, 0))] * 2,
        out_shape=[jax.ShapeDtypeStruct((B, S, LANES), F32)] * 2,
        compiler_params=_cparams(2),
        name="rope_tables",
    )(pos, inv128, sign128)


def _ffn_kernel(chunks, shift_i, h_ref, mod_ref, nw_ref, wg_ref, wu_ref, wd_ref, o_ref, act_ref):
    x = h_ref[0]
    m = mod_ref[0]
    u = _norm_mod(x, nw_ref[...], m[shift_i:shift_i + 1], m[shift_i + 1:shift_i + 2]).astype(BF16)
    for lo, hi in chunks:
        g = _dot(u, wg_ref[:, lo:hi])
        up = _dot(u, wu_ref[:, lo:hi])
        act_ref[:, lo:hi] = (_silu(g) * up).astype(BF16)
    y = _dot(act_ref[...], wd_ref[...])
    o_ref[0] = x + (0.5 * m[shift_i + 2:shift_i + 3]) * y


def _ffn(h, mod, nw, wg, wu, wd, shift_i):
    B, S, D = h.shape
    F = wg.shape[1]
    tm = min(FFN_TM, S)
    chunks, lo = [], 0
    while lo < F:
        hi = min(lo + 512, F)
        chunks.append((lo, hi))
        lo = hi
    return pl.pallas_call(
        functools.partial(_ffn_kernel, tuple(chunks), shift_i),
        grid=(B, S // tm),
        in_specs=[pl.BlockSpec((1, tm, D), lambda b, s: (b, s, 0)),
                  pl.BlockSpec((1, N_MOD, D), lambda b, s: (b, 0, 0)),
                  _const_spec((1, D)), _const_spec((D, F)), _const_spec((D, F)), _const_spec((F, D))],
        out_specs=pl.BlockSpec((1, tm, D), lambda b, s: (b, s, 0)),
        out_shape=jax.ShapeDtypeStruct((B, S, D), F32),
        scratch_shapes=[pltpu.VMEM((tm, F), BF16)],
        compiler_params=_cparams(2),
        name="ffn",
    )(h, mod, nw, wg, wu, wd)


_O_QA, _O_KA, _O_VA = 0, W_A, 2 * W_A
_O_B = P_A
_O_C = P_A + P_B_PAD
_O_G = _O_C + P_C


def _mixin_kernel(h_ref, mod_ref, nw_ref, w_ref, cos_ref, sin_ref, qkw_ref, g64_ref, mu_ref,
                  w0_ref, w2_ref, a0_ref, a2_ref, g2_ref, kk_ref, ka_ref, rk_ref, llb_ref, l1lb_ref,
                  qa_o, ka_o, va_o, r_o, k_o, v_o, kkn_o, beta_o, logw_o, g_o, bonus_o,
                  qc_o, logf_o, kc_o, ic_o, gc_o, gates_o, prev_ref):
    tm = h_ref.shape[1]
    D = h_ref.shape[2]
    x = h_ref[0]
    m = mod_ref[0]
    u = _norm_mod(x, nw_ref[...], m[3:4], m[4:5]).astype(BF16)

    def proj(lo, hi):
        return _dot(u, w_ref[:, lo:hi])

    cos4 = jnp.concatenate([cos_ref[0]] * (W_A // LANES), axis=1)
    sin4 = jnp.concatenate([sin_ref[0]] * (W_A // LANES), axis=1)
    lane = lax.broadcasted_iota(jnp.int32, (1, W_A), 1)
    first_half = (lane % DH_A) < (DH_A // 2)

    def qk_prep(p, wrow, scale):
        ss = _group_sum(p * p, g64_ref)
        n = p * lax.rsqrt(ss * (1.0 / DH_A) + EPS) * wrow
        rot = jnp.where(first_half, pltpu.roll(n, W_A - DH_A // 2, 1), pltpu.roll(n, DH_A // 2, 1))
        return ((n * cos4 + rot * sin4) * scale).astype(BF16)

    qkw = qkw_ref[...]
    qa_o[0] = qk_prep(proj(_O_QA, _O_QA + W_A), qkw[0:1], DH_A ** -0.5 * math.log2(math.e))
    ka_o[0] = qk_prep(proj(_O_KA, _O_KA + W_A), qkw[1:2], 1.0)
    va_o[0] = proj(_O_VA, _O_VA + W_A).astype(BF16)

    pb = proj(_O_B, _O_B + P_B_PAD)

    @pl.when(pl.program_id(1) == 0)
    def _():
        prev_ref[...] = jnp.zeros_like(prev_ref)

    row = lax.broadcasted_iota(jnp.int32, (tm, 1), 0)
    shifted = jnp.where(row == 0, prev_ref[...], pltpu.roll(pb, 1, 0))
    prev_ref[...] = pb[tm - 1:tm]
    pb = pb + (shifted - pb) * mu_ref[...]
    r = pb[:, 0:W_B]
    k = pb[:, W_B:2 * W_B]
    v = pb[:, 2 * W_B:3 * W_B]
    xwa = pb[:, 3 * W_B:3 * W_B + R_W + R_A]
    xg = pb[:, 3 * W_B + R_W + R_A:]
    w = -_softplus(-(w0_ref[...] + _dot(jnp.tanh(xwa).astype(BF16), w2_ref[...]))) - 0.5
    logw_o[0] = -jnp.exp(w)
    a = _sigmoid(a0_ref[...] + _dot(xwa.astype(BF16), a2_ref[...]))
    g_o[0] = _dot(_sigmoid(xg).astype(BF16), g2_ref[...]).astype(BF16)
    kk = k * kk_ref[...]
    kkn = kk / jnp.maximum(jnp.sqrt(_group_sum(kk * kk, g64_ref)), 1e-12)
    k2 = k * (1.0 + (a - 1.0) * ka_ref[...])
    r_o[0] = r.astype(BF16)
    k_o[0] = k2.astype(BF16)
    v_o[0] = v.astype(BF16)
    kkn_o[0] = kkn.astype(BF16)
    beta_o[0] = (kkn * a).astype(BF16)
    bonus_o[0] = (_group_sum(r * k2 * rk_ref[...], g64_ref) * v).astype(BF16)

    qc_o[0] = _silu(proj(_O_C, _O_C + W_C)).astype(BF16)
    fz = proj(_O_C + W_C, _O_C + 2 * W_C)
    log_sig = jnp.minimum(fz, 0.0) - jnp.log1p(jnp.exp(-jnp.abs(fz)))
    t1 = llb_ref[...]
    t2 = l1lb_ref[...] + log_sig
    logf = jnp.maximum(t1, t2) + jnp.log1p(jnp.exp(-jnp.abs(t1 - t2)))
    logf_o[0] = logf
    kc_o[0] = (1.0 - jnp.exp(logf)).astype(BF16)
    ic_o[0] = proj(_O_C + 2 * W_C, _O_C + 3 * W_C).astype(BF16)
    gc_o[0] = _silu(proj(_O_C + 3 * W_C, _O_C + 4 * W_C)).astype(BF16)

    gates_o[0] = _sigmoid(proj(_O_G, _O_G + 3 * D)).astype(BF16)


def _mixer_in(h, mod, nw, w_pad, cos, sin, qkw, g64, mu, w0, w2p, a0, a2p, g2p, k_k, k_a, r_k, llb, l1lb):
    B, S, D = h.shape
    tm = min(MIX_TM, S)
    PT = w_pad.shape[1]
    tok = lambda width: pl.BlockSpec((1, tm, width), lambda b, s: (b, s, 0))
    out_widths = [W_A] * 3 + [W_B] * 8 + [W_C] * 5 + [3 * D]
    out_dtypes = [BF16] * 3 + [BF16] * 5 + [F32, BF16, BF16] + [BF16, F32, BF16, BF16, BF16] + [BF16]
    return pl.pallas_call(
        _mixin_kernel,
        grid=(B, S // tm),
        in_specs=[tok(D), pl.BlockSpec((1, N_MOD, D), lambda b, s: (b, 0, 0)), _const_spec((1, D)),
                  _const_spec((D, PT)), tok(LANES), tok(LANES), _const_spec((2, W_A)), _const_spec((W_B, W_B)),
                  _const_spec((1, P_B_PAD)), _const_spec((1, W_B)), _const_spec((R_W + R_A, W_B)),
                  _const_spec((1, W_B)), _const_spec((R_W + R_A, W_B)), _const_spec((R_G_PAD, W_B)),
                  _const_spec((1, W_B)), _const_spec((1, W_B)), _const_spec((1, W_B)),
                  _const_spec((1, W_C)), _const_spec((1, W_C))],
        out_specs=[tok(wd) for wd in out_widths],
        out_shape=[jax.ShapeDtypeStruct((B, S, wd), dt) for wd, dt in zip(out_widths, out_dtypes)],
        scratch_shapes=[pltpu.VMEM((1, P_B_PAD), F32)],
        compiler_params=_cparams(2),
        name="mixer_in",
    )(h, mod, nw, w_pad, cos, sin, qkw, g64, mu, w0, w2p, a0, a2p, g2p, k_k, k_a, r_k, llb, l1lb)


def _attn_kernel(lam_init, q_ref, k_ref, v_ref, lq_ref, sw_ref, o_ref, qs_ref, m_ref, l_ref, acc_ref):
    t = q_ref.shape[1]
    qi = pl.program_id(2)
    q = q_ref[0]
    lane = lax.broadcasted_iota(jnp.int32, (1, LANES), 1)
    zero = jnp.zeros_like(q)
    qs_ref[0:t] = jnp.where(lane < DH_A, q, zero)
    qs_ref[t:2 * t] = jnp.where(lane < DH_A, zero, q)
    m_ref[...] = jnp.full_like(m_ref, -jnp.inf)
    l_ref[...] = jnp.zeros_like(l_ref)
    acc_ref[...] = jnp.zeros_like(acc_ref)

    def step(j, masked):
        start = pl.multiple_of(j * t, t)
        k = k_ref[0, pl.ds(start, t), :]
        v = v_ref[0, pl.ds(start, t), :]
        s = _dot_nt(qs_ref[...], k)
        if masked:
            r_i = lax.broadcasted_iota(jnp.int32, (2 * t, t), 0) % t
            c_i = lax.broadcasted_iota(jnp.int32, (2 * t, t), 1)
            s = jnp.where(c_i <= r_i, s, -jnp.inf)
        m_old = m_ref[...]
        m_new = jnp.maximum(m_old, jnp.max(s, axis=-1, keepdims=True))
        alpha = jnp.exp2(m_old - m_new)
        p = jnp.exp2(s - pltpu.repeat(m_new, t // LANES, 1))
        l_ref[...] = alpha * l_ref[...] + jnp.sum(p, axis=-1, keepdims=True)
        acc_ref[...] = alpha * acc_ref[...] + _dot(p.astype(BF16), v)
        m_ref[...] = m_new

    def body(j, carry):
        step(j, False)
        return carry

    lax.fori_loop(0, qi, body, 0)
    step(qi, True)

    lq = lq_ref[...]
    lam = (jnp.exp(jnp.sum(lq[0:1] * lq[1:2], axis=-1, keepdims=True))
           - jnp.exp(jnp.sum(lq[2:3] * lq[3:4], axis=-1, keepdims=True)) + lam_init)
    o = acc_ref[...] / l_ref[...]
    o = o[0:t] - lam * o[t:2 * t]
    ms = jnp.mean(o * o, axis=-1, keepdims=True)
    o_ref[0] = (o * lax.rsqrt(ms + EPS) * sw_ref[...] * (1.0 - lam_init)).astype(BF16)


def _diff_attention(qa, ka, va, lambda_qk, subln_w, layer_idx):
    B, S, _ = qa.shape
    t = min(ATT_T, S)
    lam_init = 0.8 - 0.6 * math.exp(-0.3 * layer_idx)
    q_spec = pl.BlockSpec((1, t, LANES), lambda b, h, i: (b, i, h))
    kv_spec = pl.BlockSpec((1, S, LANES), lambda b, h, i: (b, 0, h))
    return pl.pallas_call(
        functools.partial(_attn_kernel, lam_init),
        grid=(B, H_A, S // t),
        in_specs=[q_spec, kv_spec, kv_spec,
                  pl.BlockSpec((4, DH_A), lambda b, h, i: (0, 0)),
                  pl.BlockSpec((1, 2 * DH_A), lambda b, h, i: (0, 0))],
        out_specs=pl.BlockSpec((1, t, LANES), lambda b, h, i: (b, i, h)),
        out_shape=jax.ShapeDtypeStruct((B, S, W_A), BF16),
        scratch_shapes=[pltpu.VMEM((2 * t, LANES), BF16), pltpu.VMEM((2 * t, LANES), F32),
                        pltpu.VMEM((2 * t, LANES), F32), pltpu.VMEM((2 * t, LANES), F32)],
        compiler_params=_cparams(3),
        name="diff_attention",
    )(qa, ka, va, lambda_qk, subln_w)


def _rwkv_kernel(r_ref, k_ref, v_ref, kk_ref, beta_ref, lw_ref, g_ref, bonus_ref, lnw_ref, lnb_ref, g64_ref,
                 o_ref, state_ref):
    C = RWKV_C
    C2 = 2 * C
    n_chunks = r_ref.shape[1] // C

    @pl.when(pl.program_id(1) == 0)
    def _():
        state_ref[...] = jnp.zeros_like(state_ref)

    lane = lax.broadcasted_iota(jnp.int32, (1, LANES), 1)
    head0 = lane < N_B
    ri = lax.broadcasted_iota(jnp.int32, (C2, 1), 0)
    ci = lax.broadcasted_iota(jnp.int32, (1, C2), 1)
    same = (ri < C) == (ci < C)
    strict = same & ((ci % C) < (ri % C))
    incl = same & ((ci % C) <= (ri % C))
    same_state = (lax.broadcasted_iota(jnp.int32, (LANES, 1), 0) < N_B) == head0

    def stack(x):
        z = jnp.zeros_like(x)
        return jnp.concatenate([jnp.where(head0, x, z), jnp.where(head0, z, x)], axis=0)

    def fold(x):
        return x[0:C] + x[C:C2]

    n_pairs = H_B // 2
    chains = []
    for n in range(n_chunks):
        rows = slice(n * C, (n + 1) * C)
        lw = lw_ref[0, rows, :]
        cum = _cumsum_rows(lw, C)
        c_last = cum[C - 1:C]
        p_inv = jnp.exp(-cum)
        p_end = jnp.exp(c_last - cum)
        p_last = jnp.exp(c_last)
        k = k_ref[0, rows, :].astype(F32)
        beta = beta_ref[0, rows, :].astype(F32)
        v_all = v_ref[0, rows, :]
        rt = (r_ref[0, rows, :].astype(F32) * jnp.exp(cum)).astype(BF16)
        kt = (kk_ref[0, rows, :].astype(F32) * jnp.exp(cum - lw)).astype(BF16)
        kh = (k * p_inv).astype(BF16)
        bh = (beta * p_inv).astype(BF16)
        ke = (k * p_end).astype(BF16)
        be = (beta * p_end).astype(BF16)
        for p in range(n_pairs):
            sl = slice(p * LANES, (p + 1) * LANES)
            v = v_all[:, sl]
            chains.append(dict(
                lhs=jnp.concatenate([stack(kt[:, sl]), stack(rt[:, sl])], axis=0),
                rhs4=jnp.concatenate([kh[:, sl]] * 2 + [bh[:, sl]] * 2, axis=0),
                ktrt=jnp.concatenate([kt[:, sl], rt[:, sl]], axis=0),
                v=v, v_s=stack(v),
                kebe=jnp.concatenate([ke[:, sl], -be[:, sl]], axis=0),
                p_last=p_last[:, sl]))

    for d in chains:
        a_all = _dot_nt(d.pop("lhs"), d.pop("rhs4"))
        d["a_kk"] = jnp.where(strict, a_all[0:C2, 0:C2], 0.0).astype(BF16)
        d["a_o"] = jnp.concatenate([jnp.where(incl, a_all[C2:, 0:C2], 0.0),
                                    -jnp.where(incl, a_all[C2:, C2:], 0.0)], axis=1).astype(BF16)
        d["xm"] = -jnp.where(strict, a_all[0:C2, C2:], 0.0)
    for d in chains:
        pw = d["xm"].astype(BF16)
        d["pf"] = _dot(pw, pw)
    s = 2
    while s < C:
        for d in chains:
            pf = d["pf"]
            pw = pf.astype(BF16)
            if 2 * s >= C:
                d["xm"] = d["xm"] + pf + _dot(d["xm"].astype(BF16), pw)
            else:
                both = _dot(jnp.concatenate([pw, d["xm"].astype(BF16)], axis=0), pw)
                d["xm"] = d["xm"] + pf + both[C2:]
                d["pf"] = both[0:C2]
        s *= 2
    for d in chains:
        d["av"] = fold(_dot(d.pop("a_kk"), d["v_s"]))
        d["xm"] = d["xm"].astype(BF16)

    state = [state_ref[p] for p in range(n_pairs)]
    o_chunks = []
    for n in range(n_chunks):
        cs = chains[n * n_pairs:(n + 1) * n_pairs]
        w1r1 = [_dot_nt(d["ktrt"], st.astype(BF16)) for d, st in zip(cs, state)]
        rhs = [w[0:C] + d["av"] for d, w in zip(cs, w1r1)]
        u_b = [(x + fold(_dot(d["xm"], stack(x.astype(BF16))))).astype(BF16) for d, x in zip(cs, rhs)]
        outs = [w[C:C2] + fold(_dot(d["a_o"], jnp.concatenate([d["v_s"], stack(u)], axis=0)))
                for d, w, u in zip(cs, w1r1, u_b)]
        upd = [_dot_tn(jnp.concatenate([d["v"], u], axis=0), d["kebe"]) for d, u in zip(cs, u_b)]
        state = [st * d["p_last"] + jnp.where(same_state, x, 0.0) for d, st, x in zip(cs, state, upd)]
        o_chunks.append(jnp.concatenate(outs, axis=1))
    for p in range(n_pairs):
        state_ref[p] = state[p]

    o = jnp.concatenate(o_chunks, axis=0)
    mean = _group_sum(o, g64_ref) * (1.0 / N_B)
    d = o - mean
    var = _group_sum(d * d, g64_ref) * (1.0 / N_B)
    y = d * lax.rsqrt(var + RWKV_LN_EPS) * lnw_ref[...] + lnb_ref[...] + bonus_ref[0].astype(F32)
    o_ref[0] = (y * g_ref[0].astype(F32)).astype(BF16)


def _rwkv_scan(r, k, v, kkn, beta, logw, g, bonus, ln_w, ln_b, g64):
    B, S, W = r.shape
    t = min(RWKV_T, S)
    tok = pl.BlockSpec((1, t, W), lambda b, c: (b, c, 0))
    return pl.pallas_call(
        _rwkv_kernel,
        grid=(B, S // t),
        in_specs=[tok] * 8 + [_const_spec((1, W)), _const_spec((1, W)), _const_spec((W, W))],
        out_specs=tok,
        out_shape=jax.ShapeDtypeStruct((B, S, W), BF16),
        scratch_shapes=[pltpu.VMEM((H_B // 2, LANES, LANES), F32)],
        compiler_params=_cparams(2),
        name="rwkv7_scan",
    )(r, k, v, kkn, beta, logw, g, bonus, ln_w, ln_b, g64)


def _hgrn_kernel(q_ref, k_ref, v_ref, lf_ref, gate_ref, nw_ref, o_ref, state_ref):
    C = q_ref.shape[1]
    c = HGRN_SUB

    @pl.when(pl.program_id(1) == 0)
    def _():
        state_ref[...] = jnp.zeros_like(state_ref)

    b = _cumsum_rows(lf_ref[0], C)
    b_end = b[C - 1:C]
    q = q_ref[0].astype(F32)
    k = k_ref[0].astype(F32)
    v_all = v_ref[0]
    q_in = (q * jnp.exp(b)).astype(BF16)
    k_end = (k * jnp.exp(b_end - b)).astype(BF16)
    p_end = jnp.exp(b_end)
    lane_c = lax.broadcasted_iota(jnp.int32, (1, c), 1)
    row_c = lax.broadcasted_iota(jnp.int32, (c, 1), 0)

    outs = []
    for h in range(H_C):
        sl = slice(h * LANES, (h + 1) * LANES)
        st = state_ref[h]
        o_inter = _dot_nt(q_in[:, sl], st.astype(BF16))
        v = v_all[:, sl]
        blocks = []
        for i in range(C // c):
            rows = slice(i * c, (i + 1) * c)
            b_i = b[rows, sl]
            q_i = q[rows, sl]
            k_i = k[rows, sl]
            sc = jnp.zeros((c, c), F32)
            for j in range(c):
                e = jnp.exp(jnp.minimum(b_i - b_i[j:j + 1], 0.0))
                col = jnp.sum(q_i * e * k_i[j:j + 1], axis=-1, keepdims=True)
                sc = jnp.where(lane_c == j, col, sc)
            sc = jnp.where(lane_c <= row_c, sc, 0.0)
            o_i = _dot(sc.astype(BF16), v[rows])
            if i > 0:
                b_ref_ = b[i * c - 1:i * c, sl]
                qs = (q_i * jnp.exp(b_i - b_ref_)).astype(BF16)
                ks = (k[0:i * c, sl] * jnp.exp(b_ref_ - b[0:i * c, sl])).astype(BF16)
                o_i = o_i + _dot(_dot_nt(qs, ks).astype(BF16), v[0:i * c])
            blocks.append(o_i)
        o_h = o_inter + jnp.concatenate(blocks, axis=0)
        state_ref[h] = st * p_end[:, sl] + _dot_tn(v, k_end[:, sl])
        ms = jnp.mean(o_h * o_h, axis=-1, keepdims=True)
        outs.append(o_h * lax.rsqrt(ms + EPS) * nw_ref[...])
    o = jnp.concatenate(outs, axis=1)
    o_ref[0] = (o * gate_ref[0].astype(F32)).astype(BF16)


def _hgrn_scan(q, k, v, logf, gate, norm_w):
    B, S, W = q.shape
    C = min(HGRN_C, S)
    tok = pl.BlockSpec((1, C, W), lambda b, c: (b, c, 0))
    return pl.pallas_call(
        _hgrn_kernel,
        grid=(B, S // C),
        in_specs=[tok] * 5 + [_const_spec((1, LANES))],
        out_specs=tok,
        out_shape=jax.ShapeDtypeStruct((B, S, W), BF16),
        scratch_shapes=[pltpu.VMEM((H_C, LANES, LANES), F32)],
        compiler_params=_cparams(2),
        name="hgrn2_scan",
    )(q, k, v, logf, gate, norm_w)


def _merge_kernel(h_ref, mod_ref, oa_ref, ob_ref, oc_ref, gates_ref, wa_ref, wb_ref, wc_ref, wo_ref, o_ref):
    D = h_ref.shape[2]
    gates = gates_ref[0]
    z = (gates[:, 0:D].astype(F32) * _dot(oa_ref[0], wa_ref[...])
         + gates[:, D:2 * D].astype(F32) * _dot(ob_ref[0], wb_ref[...])
         + gates[:, 2 * D:3 * D].astype(F32) * _dot(oc_ref[0], wc_ref[...]))
    y = _dot(z.astype(BF16), wo_ref[...])
    o_ref[0] = h_ref[0] + mod_ref[0][5:6] * y


def _merge(h, mod, oa, ob, oc, gates, wa, wb, wc, wo):
    B, S, D = h.shape
    tm = min(MERGE_TM, S)
    tok = lambda width: pl.BlockSpec((1, tm, width), lambda b, s: (b, s, 0))
    return pl.pallas_call(
        _merge_kernel,
        grid=(B, S // tm),
        in_specs=[tok(D), pl.BlockSpec((1, N_MOD, D), lambda b, s: (b, 0, 0)),
                  tok(W_A), tok(W_B), tok(W_C), tok(3 * D),
                  _const_spec((W_A, D)), _const_spec((W_B, D)), _const_spec((W_C, D)), _const_spec((D, D))],
        out_specs=tok(D),
        out_shape=jax.ShapeDtypeStruct((B, S, D), F32),
        compiler_params=_cparams(2),
        name="merge",
    )(h, mod, oa, ob, oc, gates, wa, wb, wc, wo)


def _block_ones(width, group):
    i = jnp.arange(width) // group
    return (i[:, None] == i[None, :]).astype(BF16)


def kernel(x, c, positions, mod_w, mod_b, norm_w, ffn_w_gate, ffn_w_up, ffn_w_down, w_in, qk_norm_w, lambda_qk,
           subln_w, w_out_a, rwkv_mu, rwkv_w0, rwkv_w2, rwkv_a0, rwkv_a2, rwkv_g2, rwkv_k_k, rwkv_k_a, rwkv_r_k,
           rwkv_ln_w, rwkv_ln_b, w_out_b, hgrn_lower_bounds, hgrn_norm_w, w_out_c, w_out):
    B, S, D = x.shape
    L = mod_w.shape[0]
    lb = jnp.cumsum(jax.nn.softmax(hgrn_lower_bounds.astype(F32), axis=0), axis=0)
    lb = lb - lb[0]
    log_lb = jnp.log(lb)
    log1m_lb = jnp.log1p(-lb)

    mod = _modulation(c, mod_w, mod_b).reshape(L, B, N_MOD, D)
    cos, sin = _rope_tables(positions)
    g64 = _block_ones(W_B, N_B)
    pad_cols = R_G_PAD - R_G
    zrow = lambda n: jnp.zeros((n, W_B), F32)

    h = x
    for l in range(L):
        wg = ffn_w_gate[l].astype(BF16)
        wu = ffn_w_up[l].astype(BF16)
        wd = ffn_w_down[l].astype(BF16)
        h = _ffn(h, mod[l], norm_w[l, 0][None], wg[0], wu[0], wd[0], 0)

        w_pad = jnp.concatenate([w_in[l][:, :P_A + P_B], jnp.zeros((D, pad_cols), F32),
                                 w_in[l][:, P_A + P_B:]], axis=1).astype(BF16)
        mu = jnp.concatenate([rwkv_mu[l], jnp.zeros((pad_cols,), F32)])[None]
        w2p = jnp.concatenate([rwkv_w2[l], zrow(R_A)], axis=0).astype(BF16)
        a2p = jnp.concatenate([zrow(R_W), rwkv_a2[l]], axis=0).astype(BF16)
        g2p = jnp.concatenate([rwkv_g2[l], zrow(pad_cols)], axis=0).astype(BF16)
        qkw = jnp.tile(qk_norm_w[l], (1, W_A // DH_A))
        (qa, ka, va, r, k, v, kkn, beta, logw, g, bonus, qc, logf, kc, ic, gc, gates) = _mixer_in(
            h, mod[l], norm_w[l, 1][None], w_pad, cos, sin, qkw, g64, mu,
            rwkv_w0[l][None], w2p, rwkv_a0[l][None], a2p, g2p, rwkv_k_k[l][None], rwkv_k_a[l][None],
            rwkv_r_k[l].reshape(1, W_B), log_lb[l][None], log1m_lb[l][None])
        oa = _diff_attention(qa, ka, va, lambda_qk[l], subln_w[l][None], l)
        ob = _rwkv_scan(r, k, v, kkn, beta, logw, g, bonus, rwkv_ln_w[l][None], rwkv_ln_b[l][None], g64)
        oc = _hgrn_scan(qc, kc, ic, logf, gc, hgrn_norm_w[l][None])
        h = _merge(h, mod[l], oa, ob, oc, gates, w_out_a[l].astype(BF16), w_out_b[l].astype(BF16),
                   w_out_c[l].astype(BF16), w_out[l].astype(BF16))

        h = _ffn(h, mod[l], norm_w[l, 2][None], wg[1], wu[1], wd[1], 6)
    return h
```

```python
import functools
import math

import jax
import jax.numpy as jnp
from jax import lax
from jax.experimental import pallas as pl
from jax.experimental.pallas import tpu as pltpu

F32 = jnp.float32
BF16 = jnp.bfloat16

H_A, DH_A = 4, 64
W_A = H_A * 2 * DH_A
ROPE_THETA = 10000.0
H_B, N_B = 8, 64
W_B = H_B * N_B
R_W, R_A, R_G = 64, 64, 160
RWKV_LN_EPS = 64e-5
H_C, DK_C = 4, 128
W_C = H_C * DK_C
N_MOD = 9
EPS = 1e-6
P_A = 3 * W_A
P_B = 3 * W_B + R_W + R_A + R_G
P_C = 4 * W_C

LANES = 128
R_G_PAD = 256
P_B_PAD = 3 * W_B + R_W + R_A + R_G_PAD
VMEM_LIMIT = 56 * 1024 * 1024

FFN_TM = 512
MIX_TM = 256
ATT_T = 512
RWKV_C = 64
RWKV_T = 256
HGRN_C = 64
HGRN_T = 256
HGRN_SUB = 16
MERGE_TM = 512


def _cparams(n_axes):
    return pltpu.CompilerParams(dimension_semantics=("arbitrary",) * n_axes,
                                vmem_limit_bytes=VMEM_LIMIT)


def _const_spec(shape):
    nd = len(shape)
    return pl.BlockSpec(shape, lambda *_: (0,) * nd, pipeline_mode=pl.Buffered(1))


def _sigmoid(x):
    return 1.0 / (1.0 + jnp.exp(-x))


def _silu(x):
    return x * _sigmoid(x)


def _log1p_exp_neg_abs(x):
    return jnp.log(1.0 + jnp.exp(-jnp.abs(x)))


def _dot(a, b):
    return jnp.dot(a, b, preferred_element_type=F32)


def _dot_nt(a, b):
    return lax.dot_general(a, b, (((1,), (1,)), ((), ())), preferred_element_type=F32)


def _dot_tn(a, b):
    return lax.dot_general(a, b, (((0,), (0,)), ((), ())), preferred_element_type=F32)


def _group_sum(x, g_ref, split):
    g = g_ref[...]
    w = g.shape[0]

    def halves(xb):
        return jnp.concatenate([_dot(xb[:, i * w:(i + 1) * w], g) for i in range(x.shape[1] // w)], axis=1)

    hi = x.astype(BF16)
    if not split:
        return halves(hi)
    return halves(hi) + halves((x - hi.astype(F32)).astype(BF16))


def _norm_mod(x, nw, shift, scale):
    ms = jnp.mean(x * x, axis=-1, keepdims=True)
    return (x * lax.rsqrt(ms + EPS) * nw) * (1.0 + scale) + shift


def _cumsum_rows(x, n):
    row = lax.broadcasted_iota(jnp.int32, (n, 1), 0)
    s = 1
    while s < n:
        x = x + jnp.where(row >= s, pltpu.roll(x, s, 0), 0.0)
        s *= 2
    return x


def _mod_kernel(c_ref, w_ref, b_ref, o_ref):
    c = c_ref[...]
    cond = _silu(c)
    o_ref[0] = jnp.dot(cond, w_ref[0], preferred_element_type=F32,
                       precision=lax.Precision.HIGHEST) + b_ref[0]


def _modulation(c, mod_w, mod_b):
    L, D, N = mod_w.shape
    B = c.shape[0]
    tn = 1024
    return pl.pallas_call(
        _mod_kernel,
        grid=(L, N // tn),
        in_specs=[pl.BlockSpec((B, D), lambda l, j: (0, 0)),
                  pl.BlockSpec((1, D, tn), lambda l, j: (l, 0, j)),
                  pl.BlockSpec((1, 1, tn), lambda l, j: (l, 0, j))],
        out_specs=pl.BlockSpec((1, B, tn), lambda l, j: (l, 0, j)),
        out_shape=jax.ShapeDtypeStruct((L, B, N), F32),
        compiler_params=_cparams(2),
        name="modulation",
    )(c, mod_w, mod_b.reshape(L, 1, N))


def _rope_kernel(pos_ref, inv_ref, sign_ref, cos_ref, sin_ref):
    ang = pos_ref[0] * inv_ref[...]
    cos_ref[0] = jnp.cos(ang)
    sin_ref[0] = jnp.sin(ang) * sign_ref[...]


def _rope_tables(positions):
    B, S = positions.shape
    ts = 512 if S % 512 == 0 else S
    half = DH_A // 2
    inv = ROPE_THETA ** (-jnp.arange(0, DH_A, 2, dtype=F32) / DH_A)
    inv128 = jnp.tile(inv, LANES // half)[None, :]
    sign128 = jnp.tile(jnp.concatenate([-jnp.ones(half, F32), jnp.ones(half, F32)]), LANES // DH_A)[None, :]
    pos = positions.astype(F32)[..., None]
    return pl.pallas_call(
        _rope_kernel,
        grid=(B, S // ts),
        in_specs=[pl.BlockSpec((1, ts, 1), lambda b, s: (b, s, 0)),
                  pl.BlockSpec((1, LANES), lambda b, s: (0, 0)),
                  pl.BlockSpec((1, LANES), lambda b, s: (0, 0))],
        out_specs=[pl.BlockSpec((1, ts, LANES), lambda b, s: (b, s, 0))] * 2,
        out_shape=[jax.ShapeDtypeStruct((B, S, LANES), F32)] * 2,
        compiler_params=_cparams(2),
        name="rope_tables",
    )(pos, inv128, sign128)


def _ffn_kernel(chunks, shift_i, h_ref, mod_ref, nw_ref, wg_ref, wu_ref, wd_ref, o_ref, act_ref):
    x = h_ref[0]
    m = mod_ref[0]
    u = _norm_mod(x, nw_ref[...], m[shift_i:shift_i + 1], m[shift_i + 1:shift_i + 2]).astype(BF16)
    for lo, hi in chunks:
        g = _dot(u, wg_ref[:, lo:hi])
        up = _dot(u, wu_ref[:, lo:hi])
        act_ref[:, lo:hi] = (_silu(g) * up).astype(BF16)
    y = _dot(act_ref[...], wd_ref[...])
    o_ref[0] = x + (0.5 * m[shift_i + 2:shift_i + 3]) * y


def _ffn(h, mod, nw, wg, wu, wd, shift_i):
    B, S, D = h.shape
    F = wg.shape[1]
    tm = min(FFN_TM, S)
    chunks, lo = [], 0
    while lo < F:
        hi = min(lo + 512, F)
        chunks.append((lo, hi))
        lo = hi
    return pl.pallas_call(
        functools.partial(_ffn_kernel, tuple(chunks), shift_i),
        grid=(B, S // tm),
        in_specs=[pl.BlockSpec((1, tm, D), lambda b, s: (b, s, 0)),
                  pl.BlockSpec((1, N_MOD, D), lambda b, s: (b, 0, 0)),
                  _const_spec((1, D)), _const_spec((D, F)), _const_spec((D, F)), _const_spec((F, D))],
        out_specs=pl.BlockSpec((1, tm, D), lambda b, s: (b, s, 0)),
        out_shape=jax.ShapeDtypeStruct((B, S, D), F32),
        scratch_shapes=[pltpu.VMEM((tm, F), BF16)],
        compiler_params=_cparams(2),
        name="ffn",
    )(h, mod, nw, wg, wu, wd)


_O_QA, _O_KA, _O_VA = 0, W_A, 2 * W_A
_O_B = P_A
_O_C = P_A + P_B_PAD
_O_G = _O_C + P_C


_M_GATES = 0
_M_A = 3 * 1024
_M_B = _M_A + P_A
_M_C = _M_B + 7 * W_B
_M_TOTAL = _M_C + P_C


def _mixin_kernel(h_ref, mod_ref, nw_ref, w_ref, cos_ref, sin_ref, qkw_ref, g64_ref, mu_ref,
                  w0_ref, w2_ref, a0_ref, a2_ref, g2_ref, kk_ref, ka_ref, rk_ref, llb_ref, l1lb_ref,
                  mix_o, logs_o, prev_ref):
    tm = h_ref.shape[1]
    D = h_ref.shape[2]
    x = h_ref[0]
    m = mod_ref[0]
    u = _norm_mod(x, nw_ref[...], m[3:4], m[4:5]).astype(BF16)

    def put(off, val):
        mix_o[0, :, off:off + val.shape[1]] = val.astype(BF16)

    def qk_prep(p, wrow, scale):
        cos4 = jnp.concatenate([cos_ref[0]] * (W_A // LANES), axis=1)
        sin4 = jnp.concatenate([sin_ref[0]] * (W_A // LANES), axis=1)
        lane = lax.broadcasted_iota(jnp.int32, (1, W_A), 1)
        first_half = (lane % DH_A) < (DH_A // 2)
        ss = _group_sum(p * p, g64_ref, False)
        n = p * lax.rsqrt(ss * (1.0 / DH_A) + EPS) * wrow
        rot = jnp.where(first_half, pltpu.roll(n, W_A - DH_A // 2, 1), pltpu.roll(n, DH_A // 2, 1))
        return (n * cos4 + rot * sin4) * scale

    def use_qa(p):
        put(_M_A, qk_prep(p, qkw_ref[0:1], DH_A ** -0.5 * math.log2(math.e)))

    def use_ka(p):
        put(_M_A + W_A, qk_prep(p, qkw_ref[1:2], 1.0))

    def use_va(p):
        put(_M_A + 2 * W_A, p)

    def use_rwkv(pb):
        @pl.when(pl.program_id(1) == 0)
        def _():
            prev_ref[...] = jnp.zeros_like(prev_ref)

        row = lax.broadcasted_iota(jnp.int32, (tm, 1), 0)
        shifted = jnp.where(row == 0, prev_ref[...], pltpu.roll(pb, 1, 0))
        prev_ref[...] = pb[tm - 1:tm]
        pb = pb + (shifted - pb) * mu_ref[...]
        r = pb[:, 0:W_B]
        k = pb[:, W_B:2 * W_B]
        v = pb[:, 2 * W_B:3 * W_B]
        xwa = pb[:, 3 * W_B:3 * W_B + R_W + R_A]
        xg = pb[:, 3 * W_B + R_W + R_A:]
        z = w0_ref[...] + _dot(jnp.tanh(xwa).astype(BF16), w2_ref[...])
        logs_o[0, :, 0:W_B] = (-math.exp(-0.5)) * _sigmoid(z)
        a = _sigmoid(a0_ref[...] + _dot(xwa.astype(BF16), a2_ref[...]))
        kk = k * kk_ref[...]
        kkn = kk / jnp.maximum(jnp.sqrt(_group_sum(kk * kk, g64_ref, False)), 1e-12)
        k2 = k * (1.0 + (a - 1.0) * ka_ref[...])
        put(_M_B, r)
        put(_M_B + W_B, k2)
        put(_M_B + 2 * W_B, v)
        put(_M_B + 3 * W_B, kkn)
        put(_M_B + 4 * W_B, kkn * a)
        put(_M_B + 5 * W_B, _dot(_sigmoid(xg).astype(BF16), g2_ref[...]))
        put(_M_B + 6 * W_B, _group_sum(r * k2 * rk_ref[...], g64_ref, False) * v)

    def use_qc(p):
        put(_M_C, _silu(p))

    def use_fz(fz):
        log_sig = jnp.minimum(fz, 0.0) - _log1p_exp_neg_abs(fz)
        t1 = llb_ref[...]
        t2 = l1lb_ref[...] + log_sig
        logf = jnp.maximum(t1, t2) + _log1p_exp_neg_abs(t1 - t2)
        logs_o[0, :, W_B:W_B + W_C] = logf
        put(_M_C + W_C, 1.0 - jnp.exp(logf))

    def use_ic(p):
        put(_M_C + 2 * W_C, p)

    def use_gc(p):
        put(_M_C + 3 * W_C, _silu(p))

    def use_gates(j):
        return lambda p: put(_M_GATES + j * D, _sigmoid(p))

    segments = [(_O_QA, W_A, use_qa), (_O_KA, W_A, use_ka), (_O_VA, W_A, use_va), (_O_B, P_B_PAD, use_rwkv),
                (_O_C, W_C, use_qc), (_O_C + W_C, W_C, use_fz), (_O_C + 2 * W_C, W_C, use_ic),
                (_O_C + 3 * W_C, W_C, use_gc)] + [(_O_G + j * D, D, use_gates(j)) for j in range(3)]

    def proj(seg):
        return _dot(u, w_ref[:, seg[0]:seg[0] + seg[1]])

    p_next = proj(segments[0])
    for i, seg in enumerate(segments):
        p_cur = p_next
        if i + 1 < len(segments):
            p_next = proj(segments[i + 1])
        seg[2](p_cur)


def _mixer_in(h, mod, nw, w_pad, cos, sin, qkw, g64, mu, w0, w2p, a0, a2p, g2p, k_k, k_a, r_k, llb, l1lb):
    B, S, D = h.shape
    tm = min(MIX_TM, S)
    PT = w_pad.shape[1]
    tok = lambda width: pl.BlockSpec((1, tm, width), lambda b, s: (b, s, 0))
    return pl.pallas_call(
        _mixin_kernel,
        grid=(B, S // tm),
        in_specs=[tok(D), pl.BlockSpec((1, N_MOD, D), lambda b, s: (b, 0, 0)), _const_spec((1, D)),
                  _const_spec((D, PT)), tok(LANES), tok(LANES), _const_spec((2, W_A)), _const_spec(g64.shape),
                  _const_spec((1, P_B_PAD)), _const_spec((1, W_B)), _const_spec((R_W + R_A, W_B)),
                  _const_spec((1, W_B)), _const_spec((R_W + R_A, W_B)), _const_spec((R_G_PAD, W_B)),
                  _const_spec((1, W_B)), _const_spec((1, W_B)), _const_spec((1, W_B)),
                  _const_spec((1, W_C)), _const_spec((1, W_C))],
        out_specs=[tok(_M_TOTAL), tok(W_B + W_C)],
        out_shape=[jax.ShapeDtypeStruct((B, S, _M_TOTAL), BF16), jax.ShapeDtypeStruct((B, S, W_B + W_C), F32)],
        scratch_shapes=[pltpu.VMEM((1, P_B_PAD), F32)],
        compiler_params=_cparams(2),
        name="mixer_in",
    )(h, mod, nw, w_pad, cos, sin, qkw, g64, mu, w0, w2p, a0, a2p, g2p, k_k, k_a, r_k, llb, l1lb)


def _attn_kernel(lam_init, q_ref, k_ref, v_ref, lq_ref, sw_ref, o_ref, qs_ref, m_ref, l_ref, acc_ref):
    t = q_ref.shape[1]
    qi = pl.program_id(2)
    q = q_ref[0]
    lane = lax.broadcasted_iota(jnp.int32, (1, LANES), 1)
    zero = jnp.zeros_like(q)
    qs_ref[0:t] = jnp.where(lane < DH_A, q, zero)
    qs_ref[t:2 * t] = jnp.where(lane < DH_A, zero, q)
    m_ref[...] = jnp.full_like(m_ref, -jnp.inf)
    l_ref[...] = jnp.zeros_like(l_ref)
    acc_ref[...] = jnp.zeros_like(acc_ref)

    def step(j, masked):
        start = pl.multiple_of(j * t, t)
        k = k_ref[0, pl.ds(start, t), :]
        v = v_ref[0, pl.ds(start, t), :]
        s = _dot_nt(qs_ref[...], k)
        if masked:
            r_i = lax.broadcasted_iota(jnp.int32, (2 * t, t), 0) % t
            c_i = lax.broadcasted_iota(jnp.int32, (2 * t, t), 1)
            s = jnp.where(c_i <= r_i, s, -jnp.inf)
        m_old = m_ref[...]
        m_new = jnp.maximum(m_old, jnp.max(s, axis=-1, keepdims=True))
        alpha = jnp.exp2(m_old - m_new)
        p = jnp.exp2(s - pltpu.repeat(m_new, t // LANES, 1))
        l_ref[...] = alpha * l_ref[...] + jnp.sum(p, axis=-1, keepdims=True)
        acc_ref[...] = alpha * acc_ref[...] + _dot(p.astype(BF16), v)
        m_ref[...] = m_new

    def body(j, carry):
        step(j, False)
        return carry

    lax.fori_loop(0, qi, body, 0)
    step(qi, True)

    lq = lq_ref[...]
    lam = (jnp.exp(jnp.sum(lq[0:1] * lq[1:2], axis=-1, keepdims=True))
           - jnp.exp(jnp.sum(lq[2:3] * lq[3:4], axis=-1, keepdims=True)) + lam_init)
    o = acc_ref[...] / l_ref[...]
    o = o[0:t] - lam * o[t:2 * t]
    ms = jnp.mean(o * o, axis=-1, keepdims=True)
    o_ref[0] = (o * lax.rsqrt(ms + EPS) * sw_ref[...] * (1.0 - lam_init)).astype(BF16)


def _diff_attention(mix, lambda_qk, subln_w, layer_idx):
    B, S, _ = mix.shape
    t = min(ATT_T, S)
    lam_init = 0.8 - 0.6 * math.exp(-0.3 * layer_idx)
    blk = lambda off: off // LANES
    q_spec = pl.BlockSpec((1, t, LANES), lambda b, h, i: (b, i, blk(_M_A) + h))
    k_spec = pl.BlockSpec((1, S, LANES), lambda b, h, i: (b, 0, blk(_M_A + W_A) + h))
    v_spec = pl.BlockSpec((1, S, LANES), lambda b, h, i: (b, 0, blk(_M_A + 2 * W_A) + h))
    return pl.pallas_call(
        functools.partial(_attn_kernel, lam_init),
        grid=(B, H_A, S // t),
        in_specs=[q_spec, k_spec, v_spec,
                  pl.BlockSpec((4, DH_A), lambda b, h, i: (0, 0)),
                  pl.BlockSpec((1, 2 * DH_A), lambda b, h, i: (0, 0))],
        out_specs=pl.BlockSpec((1, t, LANES), lambda b, h, i: (b, i, h)),
        out_shape=jax.ShapeDtypeStruct((B, S, W_A), BF16),
        scratch_shapes=[pltpu.VMEM((2 * t, LANES), BF16), pltpu.VMEM((2 * t, LANES), F32),
                        pltpu.VMEM((2 * t, LANES), F32), pltpu.VMEM((2 * t, LANES), F32)],
        compiler_params=_cparams(3),
        name="diff_attention",
    )(mix, mix, mix, lambda_qk, subln_w)


def _rwkv_kernel(r_ref, k_ref, v_ref, kk_ref, beta_ref, lw_ref, g_ref, bonus_ref, lnw_ref, lnb_ref, g64_ref,
                 o_ref, state_ref):
    C = RWKV_C
    C2 = 2 * C
    n_chunks = r_ref.shape[1] // C

    @pl.when(pl.program_id(1) == 0)
    def _():
        state_ref[...] = jnp.zeros_like(state_ref)

    lane = lax.broadcasted_iota(jnp.int32, (1, LANES), 1)
    head0 = lane < N_B
    ri = lax.broadcasted_iota(jnp.int32, (C2, 1), 0)
    ci = lax.broadcasted_iota(jnp.int32, (1, C2), 1)
    same = (ri < C) == (ci < C)
    strict = same & ((ci % C) < (ri % C))
    incl = same & ((ci % C) <= (ri % C))
    same_state = (lax.broadcasted_iota(jnp.int32, (LANES, 1), 0) < N_B) == head0

    def stack(x):
        z = jnp.zeros_like(x)
        return jnp.concatenate([jnp.where(head0, x, z), jnp.where(head0, z, x)], axis=0)

    def fold(x):
        return x[0:C] + x[C:C2]

    n_pairs = H_B // 2
    chains = []
    for n in range(n_chunks):
        rows = slice(n * C, (n + 1) * C)
        lw = lw_ref[0, rows, :]
        cum = _cumsum_rows(lw, C)
        c_last = cum[C - 1:C]
        p_inv = jnp.exp(-cum)
        p_end = jnp.exp(c_last - cum)
        p_last = jnp.exp(c_last)
        k = k_ref[0, rows, :].astype(F32)
        beta = beta_ref[0, rows, :].astype(F32)
        v_all = v_ref[0, rows, :]
        rt = (r_ref[0, rows, :].astype(F32) * jnp.exp(cum)).astype(BF16)
        kt = (kk_ref[0, rows, :].astype(F32) * jnp.exp(cum - lw)).astype(BF16)
        kh = (k * p_inv).astype(BF16)
        bh = (beta * p_inv).astype(BF16)
        ke = (k * p_end).astype(BF16)
        be = (beta * p_end).astype(BF16)
        for p in range(n_pairs):
            sl = slice(p * LANES, (p + 1) * LANES)
            v = v_all[:, sl]
            chains.append(dict(
                lhs=jnp.concatenate([stack(kt[:, sl]), stack(rt[:, sl])], axis=0),
                rhs4=jnp.concatenate([kh[:, sl]] * 2 + [bh[:, sl]] * 2, axis=0),
                ktrt=jnp.concatenate([kt[:, sl], rt[:, sl]], axis=0),
                v=v, v_s=stack(v),
                kebe=jnp.concatenate([ke[:, sl], -be[:, sl]], axis=0),
                p_last=p_last[:, sl]))

    for d in chains:
        a_all = _dot_nt(d.pop("lhs"), d.pop("rhs4"))
        d["a_kk"] = jnp.where(strict, a_all[0:C2, 0:C2], 0.0).astype(BF16)
        d["a_o"] = jnp.concatenate([jnp.where(incl, a_all[C2:, 0:C2], 0.0),
                                    -jnp.where(incl, a_all[C2:, C2:], 0.0)], axis=1).astype(BF16)
        d["xm"] = -jnp.where(strict, a_all[0:C2, C2:], 0.0)
    for d in chains:
        pw = d["xm"].astype(BF16)
        d["pf"] = _dot(pw, pw)
    s = 2
    while s < C:
        for d in chains:
            pf = d["pf"]
            pw = pf.astype(BF16)
            if 2 * s >= C:
                d["xm"] = d["xm"] + pf + _dot(d["xm"].astype(BF16), pw)
            else:
                both = _dot(jnp.concatenate([pw, d["xm"].astype(BF16)], axis=0), pw)
                d["xm"] = d["xm"] + pf + both[C2:]
                d["pf"] = both[0:C2]
        s *= 2
    for d in chains:
        d["av"] = fold(_dot(d.pop("a_kk"), d["v_s"]))
        d["xm"] = d["xm"].astype(BF16)

    state = [state_ref[p] for p in range(n_pairs)]
    o_chunks = []
    for n in range(n_chunks):
        cs = chains[n * n_pairs:(n + 1) * n_pairs]
        w1r1 = [_dot_nt(d["ktrt"], st.astype(BF16)) for d, st in zip(cs, state)]
        rhs = [w[0:C] + d["av"] for d, w in zip(cs, w1r1)]
        u_b = [(x + fold(_dot(d["xm"], stack(x.astype(BF16))))).astype(BF16) for d, x in zip(cs, rhs)]
        outs = [w[C:C2] + fold(_dot(d["a_o"], jnp.concatenate([d["v_s"], stack(u)], axis=0)))
                for d, w, u in zip(cs, w1r1, u_b)]
        upd = [_dot_tn(jnp.concatenate([d["v"], u], axis=0), d["kebe"]) for d, u in zip(cs, u_b)]
        state = [st * d["p_last"] + jnp.where(same_state, x, 0.0) for d, st, x in zip(cs, state, upd)]
        o_chunks.append(jnp.concatenate(outs, axis=1))
    for p in range(n_pairs):
        state_ref[p] = state[p]

    o = jnp.concatenate(o_chunks, axis=0)
    mean = _group_sum(o, g64_ref, True) * (1.0 / N_B)
    d = o - mean
    var = _group_sum(d * d, g64_ref, True) * (1.0 / N_B)
    y = d * lax.rsqrt(var + RWKV_LN_EPS) * lnw_ref[...] + lnb_ref[...] + bonus_ref[0].astype(F32)
    o_ref[0] = (y * g_ref[0].astype(F32)).astype(BF16)


def _rwkv_scan(mix, logs, ln_w, ln_b, g64):
    B, S, _ = mix.shape
    W = W_B
    t = min(RWKV_T, S)
    tok = lambda j: pl.BlockSpec((1, t, W), lambda b, c: (b, c, j))
    m0 = _M_B // W
    return pl.pallas_call(
        _rwkv_kernel,
        grid=(B, S // t),
        in_specs=[tok(m0 + j) for j in range(5)] + [tok(0), tok(m0 + 5), tok(m0 + 6)]
                 + [_const_spec((1, W)), _const_spec((1, W)), _const_spec(g64.shape)],
        out_specs=tok(0),
        out_shape=jax.ShapeDtypeStruct((B, S, W), BF16),
        scratch_shapes=[pltpu.VMEM((H_B // 2, LANES, LANES), F32)],
        compiler_params=_cparams(2),
        name="rwkv7_scan",
    )(mix, mix, mix, mix, mix, logs, mix, mix, ln_w, ln_b, g64)


def _hgrn_kernel(q_ref, k_ref, v_ref, lf_ref, gate_ref, nw_ref, o_ref, state_ref):
    C = HGRN_C
    c = HGRN_SUB
    n_chunks = q_ref.shape[1] // C

    @pl.when(pl.program_id(1) == 0)
    def _():
        state_ref[...] = jnp.zeros_like(state_ref)

    lane_c = lax.broadcasted_iota(jnp.int32, (1, c), 1)
    row_c = lax.broadcasted_iota(jnp.int32, (c, 1), 0)

    units = []
    for n in range(n_chunks):
        rows = slice(n * C, (n + 1) * C)
        b = _cumsum_rows(lf_ref[0, rows, :], C) * math.log2(math.e)
        b_end = b[C - 1:C]
        q = q_ref[0, rows, :].astype(F32)
        k = k_ref[0, rows, :].astype(F32)
        v_all = v_ref[0, rows, :]
        q_in = (q * jnp.exp2(b)).astype(BF16)
        k_end = (k * jnp.exp2(b_end - b)).astype(BF16)
        p_end = jnp.exp2(b_end)
        for h in range(H_C):
            sl = slice(h * LANES, (h + 1) * LANES)
            units.append(dict(b=b[:, sl], q=q[:, sl], k=k[:, sl], v=v_all[:, sl], q_in=q_in[:, sl],
                              k_end=k_end[:, sl], p_end=p_end[:, sl]))

    for d in units:
        d["cross"] = []
        for i in range(1, C // c):
            r0 = i * c
            b_ref_ = d["b"][r0 - 1:r0]
            qs = (d["q"][r0:r0 + c] * jnp.exp2(d["b"][r0:r0 + c] - b_ref_)).astype(BF16)
            ks = (d["k"][0:r0] * jnp.exp2(b_ref_ - d["b"][0:r0])).astype(BF16)
            d["cross"].append(_dot_nt(qs, ks).astype(BF16))
    for d in units:
        d["diag"] = []
        for i in range(C // c):
            r0 = i * c
            b_i = d["b"][r0:r0 + c]
            q_i = d["q"][r0:r0 + c]
            k_i = d["k"][r0:r0 + c]
            sc = jnp.zeros((c, c), F32)
            for j in range(c):
                e = jnp.exp2(b_i - b_i[j:j + 1])
                col = jnp.sum(q_i * e * k_i[j:j + 1], axis=-1, keepdims=True)
                sc = jnp.where(lane_c == j, col, sc)
            d["diag"].append(jnp.where(lane_c <= row_c, sc, 0.0).astype(BF16))
    for d in units:
        blocks = []
        for i in range(C // c):
            r0 = i * c
            o_i = _dot(d["diag"][i], d["v"][r0:r0 + c])
            if i > 0:
                o_i = o_i + _dot(d["cross"][i - 1], d["v"][0:r0])
            blocks.append(o_i)
        d["intra"] = jnp.concatenate(blocks, axis=0)

    state = [state_ref[h] for h in range(H_C)]
    o_chunks = []
    for n in range(n_chunks):
        us = units[n * H_C:(n + 1) * H_C]
        o_heads = [_dot_nt(d["q_in"], st.astype(BF16)) + d["intra"] for d, st in zip(us, state)]
        state = [st * d["p_end"] + _dot_tn(d["v"], d["k_end"]) for d, st in zip(us, state)]
        o_chunks.append(jnp.concatenate(
            [o_h * lax.rsqrt(jnp.mean(o_h * o_h, axis=-1, keepdims=True) + EPS) * nw_ref[...] for o_h in o_heads],
            axis=1))
    for h in range(H_C):
        state_ref[h] = state[h]
    o = jnp.concatenate(o_chunks, axis=0)
    o_ref[0] = (o * gate_ref[0].astype(F32)).astype(BF16)


def _hgrn_scan(mix, logs, norm_w):
    B, S, _ = mix.shape
    W = W_C
    t = min(HGRN_T, S)
    tok = lambda j: pl.BlockSpec((1, t, W), lambda b, c: (b, c, j))
    m0 = _M_C // W
    return pl.pallas_call(
        _hgrn_kernel,
        grid=(B, S // t),
        in_specs=[tok(m0), tok(m0 + 1), tok(m0 + 2), tok(W_B // W), tok(m0 + 3), _const_spec((1, LANES))],
        out_specs=tok(0),
        out_shape=jax.ShapeDtypeStruct((B, S, W), BF16),
        scratch_shapes=[pltpu.VMEM((H_C, LANES, LANES), F32)],
        compiler_params=_cparams(2),
        name="hgrn2_scan",
    )(mix, mix, mix, logs, mix, norm_w)


def _merge_kernel(h_ref, mod_ref, oa_ref, ob_ref, oc_ref, gates_ref, wa_ref, wb_ref, wc_ref, wo_ref, o_ref):
    D = h_ref.shape[2]
    gates = gates_ref[0]
    z = (gates[:, 0:D].astype(F32) * _dot(oa_ref[0], wa_ref[...])
         + gates[:, D:2 * D].astype(F32) * _dot(ob_ref[0], wb_ref[...])
         + gates[:, 2 * D:3 * D].astype(F32) * _dot(oc_ref[0], wc_ref[...]))
    y = _dot(z.astype(BF16), wo_ref[...])
    o_ref[0] = h_ref[0] + mod_ref[0][5:6] * y


def _merge(h, mod, oa, ob, oc, mix, wa, wb, wc, wo):
    B, S, D = h.shape
    tm = min(MERGE_TM, S)
    tok = lambda width: pl.BlockSpec((1, tm, width), lambda b, s: (b, s, 0))
    return pl.pallas_call(
        _merge_kernel,
        grid=(B, S // tm),
        in_specs=[tok(D), pl.BlockSpec((1, N_MOD, D), lambda b, s: (b, 0, 0)),
                  tok(W_A), tok(W_B), tok(W_C), tok(3 * D),
                  _const_spec((W_A, D)), _const_spec((W_B, D)), _const_spec((W_C, D)), _const_spec((D, D))],
        out_specs=tok(D),
        out_shape=jax.ShapeDtypeStruct((B, S, D), F32),
        compiler_params=_cparams(2),
        name="merge",
    )(h, mod, oa, ob, oc, mix, wa, wb, wc, wo)


def _block_ones(width, group):
    i = jnp.arange(width) // group
    return (i[:, None] == i[None, :]).astype(BF16)


def kernel(x, c, positions, mod_w, mod_b, norm_w, ffn_w_gate, ffn_w_up, ffn_w_down, w_in, qk_norm_w, lambda_qk,
           subln_w, w_out_a, rwkv_mu, rwkv_w0, rwkv_w2, rwkv_a0, rwkv_a2, rwkv_g2, rwkv_k_k, rwkv_k_a, rwkv_r_k,
           rwkv_ln_w, rwkv_ln_b, w_out_b, hgrn_lower_bounds, hgrn_norm_w, w_out_c, w_out):
    B, S, D = x.shape
    L = mod_w.shape[0]
    lb = jnp.cumsum(jax.nn.softmax(hgrn_lower_bounds.astype(F32), axis=0), axis=0)
    lb = lb - lb[0]
    log_lb = jnp.log(lb)
    log1m_lb = jnp.log1p(-lb)

    mod = _modulation(c, mod_w, mod_b).reshape(L, B, N_MOD, D)
    cos, sin = _rope_tables(positions)
    g64 = _block_ones(2 * LANES, N_B)
    pad_cols = R_G_PAD - R_G
    zrow = lambda n: jnp.zeros((n, W_B), F32)

    h = x
    for l in range(L):
        wg = ffn_w_gate[l].astype(BF16)
        wu = ffn_w_up[l].astype(BF16)
        wd = ffn_w_down[l].astype(BF16)
        h = _ffn(h, mod[l], norm_w[l, 0][None], wg[0], wu[0], wd[0], 0)

        w_pad = jnp.concatenate([w_in[l][:, :P_A + P_B], jnp.zeros((D, pad_cols), F32),
                                 w_in[l][:, P_A + P_B:]], axis=1).astype(BF16)
        mu = jnp.concatenate([rwkv_mu[l], jnp.zeros((pad_cols,), F32)])[None]
        w2p = jnp.concatenate([rwkv_w2[l], zrow(R_A)], axis=0).astype(BF16)
        a2p = jnp.concatenate([zrow(R_W), rwkv_a2[l]], axis=0).astype(BF16)
        g2p = jnp.concatenate([rwkv_g2[l], zrow(pad_cols)], axis=0).astype(BF16)
        qkw = jnp.tile(qk_norm_w[l], (1, W_A // DH_A))
        mix, logs = _mixer_in(
            h, mod[l], norm_w[l, 1][None], w_pad, cos, sin, qkw, g64, mu,
            rwkv_w0[l][None], w2p, rwkv_a0[l][None], a2p, g2p, rwkv_k_k[l][None], rwkv_k_a[l][None],
            rwkv_r_k[l].reshape(1, W_B), log_lb[l][None], log1m_lb[l][None])
        oa = _diff_attention(mix, lambda_qk[l], subln_w[l][None], l)
        ob = _rwkv_scan(mix, logs, rwkv_ln_w[l][None], rwkv_ln_b[l][None], g64)
        oc = _hgrn_scan(mix, logs, hgrn_norm_w[l][None])
        h = _merge(h, mod[l], oa, ob, oc, mix, w_out_a[l].astype(BF16), w_out_b[l].astype(BF16),
                   w_out_c[l].astype(BF16), w_out[l].astype(BF16))

        h = _ffn(h, mod[l], norm_w[l, 2][None], wg[1], wu[1], wd[1], 6)
    return h
```

```python
import functools
import math

import jax
import jax.numpy as jnp
from jax import lax
from jax.experimental import pallas as pl
from jax.experimental.pallas import tpu as pltpu

F32 = jnp.float32
BF16 = jnp.bfloat16

H_A, DH_A = 4, 64
W_A = H_A * 2 * DH_A
ROPE_THETA = 10000.0
H_B, N_B = 8, 64
W_B = H_B * N_B
R_W, R_A, R_G = 64, 64, 160
RWKV_LN_EPS = 64e-5
H_C, DK_C = 4, 128
W_C = H_C * DK_C
N_MOD = 9
EPS = 1e-6
P_A = 3 * W_A
P_B = 3 * W_B + R_W + R_A + R_G
P_C = 4 * W_C

LANES = 128
R_G_PAD = 256
P_B_PAD = 3 * W_B + R_W + R_A + R_G_PAD
VMEM_LIMIT = 56 * 1024 * 1024

FFN_TM = 512
MIX_TM = 256
ATT_T = 512
RWKV_C = 64
RWKV_T = 128
RWKV_NB = 2
HGRN_C = 64
HGRN_T = 256
HGRN_SUB = 16
MERGE_TM = 512


def _cparams(n_axes):
    return pltpu.CompilerParams(dimension_semantics=("arbitrary",) * n_axes,
                                vmem_limit_bytes=VMEM_LIMIT)


def _const_spec(shape):
    nd = len(shape)
    return pl.BlockSpec(shape, lambda *_: (0,) * nd, pipeline_mode=pl.Buffered(1))


def _sigmoid(x):
    return 1.0 / (1.0 + jnp.exp(-x))


def _silu(x):
    return x * _sigmoid(x)


def _log1p_exp_neg_abs(x):
    return jnp.log(1.0 + jnp.exp(-jnp.abs(x)))


def _dot(a, b):
    return jnp.dot(a, b, preferred_element_type=F32)


def _dot_nt(a, b):
    return lax.dot_general(a, b, (((1,), (1,)), ((), ())), preferred_element_type=F32)


def _dot_tn(a, b):
    return lax.dot_general(a, b, (((0,), (0,)), ((), ())), preferred_element_type=F32)


def _group_sum(x, g_ref, split):
    g = g_ref[...]
    w = g.shape[0]

    def halves(xb):
        return jnp.concatenate([_dot(xb[:, i * w:(i + 1) * w], g) for i in range(x.shape[1] // w)], axis=1)

    hi = x.astype(BF16)
    if not split:
        return halves(hi)
    return halves(hi) + halves((x - hi.astype(F32)).astype(BF16))


def _norm_mod(x, nw, shift, scale):
    ms = jnp.mean(x * x, axis=-1, keepdims=True)
    return (x * lax.rsqrt(ms + EPS) * nw) * (1.0 + scale) + shift


def _cumsum_rows(x, n):
    row = lax.broadcasted_iota(jnp.int32, (n, 1), 0)
    s = 1
    while s < n:
        x = x + jnp.where(row >= s, pltpu.roll(x, s, 0), 0.0)
        s *= 2
    return x


def _mod_kernel(c_ref, w_ref, b_ref, o_ref):
    c = c_ref[...]
    cond = _silu(c)
    o_ref[0] = jnp.dot(cond, w_ref[0], preferred_element_type=F32,
                       precision=lax.Precision.HIGHEST) + b_ref[0]


def _modulation(c, mod_w, mod_b):
    L, D, N = mod_w.shape
    B = c.shape[0]
    tn = 1024
    return pl.pallas_call(
        _mod_kernel,
        grid=(L, N // tn),
        in_specs=[pl.BlockSpec((B, D), lambda l, j: (0, 0)),
                  pl.BlockSpec((1, D, tn), lambda l, j: (l, 0, j)),
                  pl.BlockSpec((1, 1, tn), lambda l, j: (l, 0, j))],
        out_specs=pl.BlockSpec((1, B, tn), lambda l, j: (l, 0, j)),
        out_shape=jax.ShapeDtypeStruct((L, B, N), F32),
        compiler_params=_cparams(2),
        name="modulation",
    )(c, mod_w, mod_b.reshape(L, 1, N))


def _rope_kernel(pos_ref, inv_ref, sign_ref, cos_ref, sin_ref):
    ang = pos_ref[0] * inv_ref[...]
    cos_ref[0] = jnp.cos(ang)
    sin_ref[0] = jnp.sin(ang) * sign_ref[...]


def _rope_tables(positions):
    B, S = positions.shape
    ts = 512 if S % 512 == 0 else S
    half = DH_A // 2
    inv = ROPE_THETA ** (-jnp.arange(0, DH_A, 2, dtype=F32) / DH_A)
    inv128 = jnp.tile(inv, LANES // half)[None, :]
    sign128 = jnp.tile(jnp.concatenate([-jnp.ones(half, F32), jnp.ones(half, F32)]), LANES // DH_A)[None, :]
    pos = positions.astype(F32)[..., None]
    return pl.pallas_call(
        _rope_kernel,
        grid=(B, S // ts),
        in_specs=[pl.BlockSpec((1, ts, 1), lambda b, s: (b, s, 0)),
                  pl.BlockSpec((1, LANES), lambda b, s: (0, 0)),
                  pl.BlockSpec((1, LANES), lambda b, s: (0, 0))],
        out_specs=[pl.BlockSpec((1, ts, LANES), lambda b, s: (b, s, 0))] * 2,
        out_shape=[jax.ShapeDtypeStruct((B, S, LANES), F32)] * 2,
        compiler_params=_cparams(2),
        name="rope_tables",
    )(pos, inv128, sign128)


def _ffn_kernel(chunks, shift_i, h_ref, mod_ref, nw_ref, wg_ref, wu_ref, wd_ref, o_ref, act_ref):
    x = h_ref[0]
    m = mod_ref[0]
    u = _norm_mod(x, nw_ref[...], m[shift_i:shift_i + 1], m[shift_i + 1:shift_i + 2]).astype(BF16)
    for lo, hi in chunks:
        g = _dot(u, wg_ref[:, lo:hi])
        up = _dot(u, wu_ref[:, lo:hi])
        act_ref[:, lo:hi] = (_silu(g) * up).astype(BF16)
    y = _dot(act_ref[...], wd_ref[...])
    o_ref[0] = x + (0.5 * m[shift_i + 2:shift_i + 3]) * y


def _ffn(h, mod, nw, wg, wu, wd, shift_i):
    B, S, D = h.shape
    F = wg.shape[1]
    tm = min(FFN_TM, S)
    chunks, lo = [], 0
    while lo < F:
        hi = min(lo + 512, F)
        chunks.append((lo, hi))
        lo = hi
    return pl.pallas_call(
        functools.partial(_ffn_kernel, tuple(chunks), shift_i),
        grid=(B, S // tm),
        in_specs=[pl.BlockSpec((1, tm, D), lambda b, s: (b, s, 0)),
                  pl.BlockSpec((1, N_MOD, D), lambda b, s: (b, 0, 0)),
                  _const_spec((1, D)), _const_spec((D, F)), _const_spec((D, F)), _const_spec((F, D))],
        out_specs=pl.BlockSpec((1, tm, D), lambda b, s: (b, s, 0)),
        out_shape=jax.ShapeDtypeStruct((B, S, D), F32),
        scratch_shapes=[pltpu.VMEM((tm, F), BF16)],
        compiler_params=_cparams(2),
        name="ffn",
    )(h, mod, nw, wg, wu, wd)


_O_QA, _O_KA, _O_VA = 0, W_A, 2 * W_A
_O_B = P_A
_O_C = P_A + P_B_PAD
_O_G = _O_C + P_C


_M_GATES = 0
_M_A = 3 * 1024
_M_B = _M_A + P_A
_M_C = _M_B + 7 * W_B
_M_TOTAL = _M_C + P_C


def _mixin_kernel(h_ref, mod_ref, nw_ref, w_ref, cos_ref, sin_ref, qkw_ref, g64_ref, mu_ref,
                  w0_ref, w2_ref, a0_ref, a2_ref, g2_ref, kk_ref, ka_ref, rk_ref, llb_ref, l1lb_ref,
                  mix_o, logs_o, prev_ref):
    tm = h_ref.shape[1]
    D = h_ref.shape[2]
    x = h_ref[0]
    m = mod_ref[0]
    u = _norm_mod(x, nw_ref[...], m[3:4], m[4:5]).astype(BF16)

    def put(off, val):
        mix_o[0, :, off:off + val.shape[1]] = val.astype(BF16)

    def qk_prep(p, wrow, scale):
        cos4 = jnp.concatenate([cos_ref[0]] * (W_A // LANES), axis=1)
        sin4 = jnp.concatenate([sin_ref[0]] * (W_A // LANES), axis=1)
        lane = lax.broadcasted_iota(jnp.int32, (1, W_A), 1)
        first_half = (lane % DH_A) < (DH_A // 2)
        ss = _group_sum(p * p, g64_ref, False)
        n = p * lax.rsqrt(ss * (1.0 / DH_A) + EPS) * wrow
        rot = jnp.where(first_half, pltpu.roll(n, W_A - DH_A // 2, 1), pltpu.roll(n, DH_A // 2, 1))
        return (n * cos4 + rot * sin4) * scale

    def use_qa(p):
        put(_M_A, qk_prep(p, qkw_ref[0:1], DH_A ** -0.5 * math.log2(math.e)))

    def use_ka(p):
        put(_M_A + W_A, qk_prep(p, qkw_ref[1:2], 1.0))

    def use_va(p):
        put(_M_A + 2 * W_A, p)

    rw = {}

    def use_rwkv_shift(pb):
        @pl.when(pl.program_id(1) == 0)
        def _():
            prev_ref[...] = jnp.zeros_like(prev_ref)

        row = lax.broadcasted_iota(jnp.int32, (tm, 1), 0)
        shifted = jnp.where(row == 0, prev_ref[...], pltpu.roll(pb, 1, 0))
        prev_ref[...] = pb[tm - 1:tm]
        pb = pb + (shifted - pb) * mu_ref[...]
        rw["r"] = pb[:, 0:W_B]
        rw["k"] = pb[:, W_B:2 * W_B]
        rw["v"] = pb[:, 2 * W_B:3 * W_B]
        rw["xwa"] = pb[:, 3 * W_B:3 * W_B + R_W + R_A]
        rw["xg"] = pb[:, 3 * W_B + R_W + R_A:]
        put(_M_B, rw["r"])
        put(_M_B + 2 * W_B, rw["v"])

    def use_rwkv_loras():
        xwa = rw["xwa"]
        z = w0_ref[...] + _dot(jnp.tanh(xwa).astype(BF16), w2_ref[...])
        logs_o[0, :, 0:W_B] = (-math.exp(-0.5)) * _sigmoid(z)
        rw["a"] = _sigmoid(a0_ref[...] + _dot(xwa.astype(BF16), a2_ref[...]))
        put(_M_B + 5 * W_B, _dot(_sigmoid(rw["xg"]).astype(BF16), g2_ref[...]))

    def use_rwkv_keys():
        k, a = rw["k"], rw["a"]
        kk = k * kk_ref[...]
        kkn = kk / jnp.maximum(jnp.sqrt(_group_sum(kk * kk, g64_ref, False)), 1e-12)
        rw["k2"] = k * (1.0 + (a - 1.0) * ka_ref[...])
        put(_M_B + W_B, rw["k2"])
        put(_M_B + 3 * W_B, kkn)
        put(_M_B + 4 * W_B, kkn * a)

    def use_rwkv_bonus():
        put(_M_B + 6 * W_B, _group_sum(rw["r"] * rw["k2"] * rk_ref[...], g64_ref, False) * rw["v"])

    def use_qc(p):
        put(_M_C, _silu(p))

    def use_fz(fz):
        log_sig = jnp.minimum(fz, 0.0) - _log1p_exp_neg_abs(fz)
        t1 = llb_ref[...]
        t2 = l1lb_ref[...] + log_sig
        logf = jnp.maximum(t1, t2) + _log1p_exp_neg_abs(t1 - t2)
        logs_o[0, :, W_B:W_B + W_C] = logf
        put(_M_C + W_C, 1.0 - jnp.exp(logf))

    def use_ic(p):
        put(_M_C + 2 * W_C, p)

    def use_gc(p):
        put(_M_C + 3 * W_C, _silu(p))

    def use_gates(j):
        return lambda p: put(_M_GATES + j * D, _sigmoid(p))

    seg = {"qa": (_O_QA, W_A), "ka": (_O_KA, W_A), "va": (_O_VA, W_A), "b": (_O_B, P_B_PAD),
           "qc": (_O_C, W_C), "fz": (_O_C + W_C, W_C), "ic": (_O_C + 2 * W_C, W_C), "gc": (_O_C + 3 * W_C, W_C),
           "g0": (_O_G, D), "g1": (_O_G + D, D), "g2": (_O_G + 2 * D, D)}
    p = {}

    def proj(name):
        lo, width = seg[name]
        p[name] = _dot(u, w_ref[:, lo:lo + width])

    proj("qa")
    proj("ka")
    use_qa(p.pop("qa"))
    proj("va")
    use_ka(p.pop("ka"))
    proj("b")
    use_va(p.pop("va"))
    proj("qc")
    use_rwkv_shift(p.pop("b"))
    proj("fz")
    use_rwkv_loras()
    proj("ic")
    use_rwkv_keys()
    proj("gc")
    use_rwkv_bonus()
    use_qc(p.pop("qc"))
    proj("g0")
    use_fz(p.pop("fz"))
    use_ic(p.pop("ic"))
    proj("g1")
    use_gc(p.pop("gc"))
    use_gates(0)(p.pop("g0"))
    proj("g2")
    use_gates(1)(p.pop("g1"))
    use_gates(2)(p.pop("g2"))


def _mixer_in(h, mod, nw, w_pad, cos, sin, qkw, g64, mu, w0, w2p, a0, a2p, g2p, k_k, k_a, r_k, llb, l1lb):
    B, S, D = h.shape
    tm = min(MIX_TM, S)
    PT = w_pad.shape[1]
    tok = lambda width: pl.BlockSpec((1, tm, width), lambda b, s: (b, s, 0))
    return pl.pallas_call(
        _mixin_kernel,
        grid=(B, S // tm),
        in_specs=[tok(D), pl.BlockSpec((1, N_MOD, D), lambda b, s: (b, 0, 0)), _const_spec((1, D)),
                  _const_spec((D, PT)), tok(LANES), tok(LANES), _const_spec((2, W_A)), _const_spec(g64.shape),
                  _const_spec((1, P_B_PAD)), _const_spec((1, W_B)), _const_spec((R_W + R_A, W_B)),
                  _const_spec((1, W_B)), _const_spec((R_W + R_A, W_B)), _const_spec((R_G_PAD, W_B)),
                  _const_spec((1, W_B)), _const_spec((1, W_B)), _const_spec((1, W_B)),
                  _const_spec((1, W_C)), _const_spec((1, W_C))],
        out_specs=[tok(_M_TOTAL), tok(W_B + W_C)],
        out_shape=[jax.ShapeDtypeStruct((B, S, _M_TOTAL), BF16), jax.ShapeDtypeStruct((B, S, W_B + W_C), F32)],
        scratch_shapes=[pltpu.VMEM((1, P_B_PAD), F32)],
        compiler_params=_cparams(2),
        name="mixer_in",
    )(h, mod, nw, w_pad, cos, sin, qkw, g64, mu, w0, w2p, a0, a2p, g2p, k_k, k_a, r_k, llb, l1lb)


def _attn_kernel(lam_init, q_ref, k_ref, v_ref, lq_ref, sw_ref, o_ref, qs_ref, m_ref, l_ref, acc_ref):
    t = q_ref.shape[1]
    qi = pl.program_id(2)
    q = q_ref[0]
    lane = lax.broadcasted_iota(jnp.int32, (1, LANES), 1)
    zero = jnp.zeros_like(q)
    qs_ref[0:t] = jnp.where(lane < DH_A, q, zero)
    qs_ref[t:2 * t] = jnp.where(lane < DH_A, zero, q)
    m_ref[...] = jnp.full_like(m_ref, -jnp.inf)
    l_ref[...] = jnp.zeros_like(l_ref)
    acc_ref[...] = jnp.zeros_like(acc_ref)

    def step(j, masked):
        start = pl.multiple_of(j * t, t)
        k = k_ref[0, pl.ds(start, t), :]
        v = v_ref[0, pl.ds(start, t), :]
        s = _dot_nt(qs_ref[...], k)
        if masked:
            r_i = lax.broadcasted_iota(jnp.int32, (2 * t, t), 0) % t
            c_i = lax.broadcasted_iota(jnp.int32, (2 * t, t), 1)
            s = jnp.where(c_i <= r_i, s, -jnp.inf)
        m_old = m_ref[...]
        m_new = jnp.maximum(m_old, jnp.max(s, axis=-1, keepdims=True))
        alpha = jnp.exp2(m_old - m_new)
        p = jnp.exp2(s - jnp.concatenate([m_new] * (t // LANES), axis=1))
        l_ref[...] = alpha * l_ref[...] + jnp.sum(p, axis=-1, keepdims=True)
        acc_ref[...] = alpha * acc_ref[...] + _dot(p.astype(BF16), v)
        m_ref[...] = m_new

    def body(j, carry):
        step(j, False)
        return carry

    lax.fori_loop(0, qi, body, 0)
    step(qi, True)

    lq = lq_ref[...]
    lam = (jnp.exp(jnp.sum(lq[0:1] * lq[1:2], axis=-1, keepdims=True))
           - jnp.exp(jnp.sum(lq[2:3] * lq[3:4], axis=-1, keepdims=True)) + lam_init)
    o = acc_ref[...] / l_ref[...]
    o = o[0:t] - lam * o[t:2 * t]
    ms = jnp.mean(o * o, axis=-1, keepdims=True)
    o_ref[0] = (o * lax.rsqrt(ms + EPS) * sw_ref[...] * (1.0 - lam_init)).astype(BF16)


def _diff_attention(mix, lambda_qk, subln_w, layer_idx):
    B, S, _ = mix.shape
    t = min(ATT_T, S)
    lam_init = 0.8 - 0.6 * math.exp(-0.3 * layer_idx)
    blk = lambda off: off // LANES
    q_spec = pl.BlockSpec((1, t, LANES), lambda b, h, i: (b, i, blk(_M_A) + h))
    k_spec = pl.BlockSpec((1, S, LANES), lambda b, h, i: (b, 0, blk(_M_A + W_A) + h))
    v_spec = pl.BlockSpec((1, S, LANES), lambda b, h, i: (b, 0, blk(_M_A + 2 * W_A) + h))
    return pl.pallas_call(
        functools.partial(_attn_kernel, lam_init),
        grid=(B, H_A, S // t),
        in_specs=[q_spec, k_spec, v_spec,
                  pl.BlockSpec((4, DH_A), lambda b, h, i: (0, 0)),
                  pl.BlockSpec((1, 2 * DH_A), lambda b, h, i: (0, 0))],
        out_specs=pl.BlockSpec((1, t, LANES), lambda b, h, i: (b, i, h)),
        out_shape=jax.ShapeDtypeStruct((B, S, W_A), BF16),
        scratch_shapes=[pltpu.VMEM((2 * t, LANES), BF16), pltpu.VMEM((2 * t, LANES), F32),
                        pltpu.VMEM((2 * t, LANES), F32), pltpu.VMEM((2 * t, LANES), F32)],
        compiler_params=_cparams(3),
        name="diff_attention",
    )(mix, mix, mix, lambda_qk, subln_w)


def _rwkv_kernel(r_ref, k_ref, v_ref, kk_ref, beta_ref, lw_ref, g_ref, bonus_ref, lnw_ref, lnb_ref, g64_ref,
                 o_ref, state_ref):
    C = RWKV_C
    C2 = 2 * C
    n_chunks = r_ref.shape[1] // C

    @pl.when(pl.program_id(1) == 0)
    def _():
        state_ref[...] = jnp.zeros_like(state_ref)

    lane = lax.broadcasted_iota(jnp.int32, (1, LANES), 1)
    head0 = lane < N_B
    ri = lax.broadcasted_iota(jnp.int32, (C, 1), 0)
    ci = lax.broadcasted_iota(jnp.int32, (1, C2), 1) % C
    strict = ci < ri
    incl = ci <= ri
    same_state = (lax.broadcasted_iota(jnp.int32, (LANES, 1), 0) < N_B) == head0

    def stack(x):
        z = jnp.zeros_like(x)
        return jnp.concatenate([jnp.where(head0, x, z), jnp.where(head0, z, x)], axis=0)

    n_pairs = H_B // 2
    n_seq = r_ref.shape[0]
    chains = []
    for bi, n in [(bi, n) for n in range(n_chunks) for bi in range(n_seq)]:
        rows = slice(n * C, (n + 1) * C)
        lw = lw_ref[bi, rows, :]
        cum = _cumsum_rows(lw, C)
        c_last = cum[C - 1:C]
        p_inv = jnp.exp(-cum)
        p_end = jnp.exp(c_last - cum)
        p_last = jnp.exp(c_last)
        k = k_ref[bi, rows, :].astype(F32)
        beta = beta_ref[bi, rows, :].astype(F32)
        v_all = v_ref[bi, rows, :]
        rt = (r_ref[bi, rows, :].astype(F32) * jnp.exp(cum)).astype(BF16)
        kt = (kk_ref[bi, rows, :].astype(F32) * jnp.exp(cum - lw)).astype(BF16)
        kh = (k * p_inv).astype(BF16)
        bh = (beta * p_inv).astype(BF16)
        ke = (k * p_end).astype(BF16)
        be = (beta * p_end).astype(BF16)
        for p in range(n_pairs):
            sl = slice(p * LANES, (p + 1) * LANES)
            v = v_all[:, sl]
            chains.append(dict(
                ktrt=jnp.concatenate([kt[:, sl], rt[:, sl]], axis=0),
                rhs4=jnp.concatenate([stack(kh[:, sl]), stack(bh[:, sl])], axis=0),
                v=v, v_s=stack(v),
                kebe=jnp.concatenate([ke[:, sl], -be[:, sl]], axis=0),
                p_last=p_last[:, sl]))

    for d in chains:
        a_all = _dot_nt(d["ktrt"], d.pop("rhs4"))
        d["a_kk"] = jnp.where(strict, a_all[0:C, 0:C2], 0.0).astype(BF16)
        d["a_o"] = jnp.concatenate([jnp.where(incl, a_all[C:, 0:C2], 0.0),
                                    -jnp.where(incl, a_all[C:, C2:], 0.0)], axis=1).astype(BF16)
        d["xm"] = -jnp.where(strict, a_all[0:C, C2:], 0.0)
    for d in chains:
        pw = d["xm"].astype(BF16)
        d["pf"] = _dot(pw, stack(pw))
    s = 2
    while s < C:
        for d in chains:
            pf = d["pf"]
            pw = pf.astype(BF16)
            xs = stack(d["xm"].astype(BF16))
            if 2 * s >= C:
                d["xm"] = d["xm"] + pf + _dot(pw, xs)
            else:
                both = _dot(pw, jnp.concatenate([stack(pw), xs], axis=1))
                d["xm"] = d["xm"] + pf + both[:, C2:]
                d["pf"] = both[:, 0:C2]
        s *= 2
    for d in chains:
        d["av"] = _dot(d.pop("a_kk"), d["v_s"])
        d["xm"] = d["xm"].astype(BF16)

    n_state = n_seq * n_pairs
    state = [state_ref[i] for i in range(n_state)]
    o_chunks = []
    for n in range(n_chunks):
        cs = chains[n * n_state:(n + 1) * n_state]
        w1r1 = [_dot_nt(d["ktrt"], st.astype(BF16)) for d, st in zip(cs, state)]
        rhs = [w[0:C] + d["av"] for d, w in zip(cs, w1r1)]
        u_b = [(x + _dot(d["xm"], stack(x.astype(BF16)))).astype(BF16) for d, x in zip(cs, rhs)]
        outs = [w[C:C2] + _dot(d["a_o"], jnp.concatenate([d["v_s"], stack(u)], axis=0))
                for d, w, u in zip(cs, w1r1, u_b)]
        upd = [_dot_tn(jnp.concatenate([d["v"], u], axis=0), d["kebe"]) for d, u in zip(cs, u_b)]
        state = [st * d["p_last"] + jnp.where(same_state, x, 0.0) for d, st, x in zip(cs, state, upd)]
        o_chunks.append([jnp.concatenate(outs[bi * n_pairs:(bi + 1) * n_pairs], axis=1) for bi in range(n_seq)])
    for i in range(n_state):
        state_ref[i] = state[i]

    for bi in range(n_seq):
        o = jnp.concatenate([oc[bi] for oc in o_chunks], axis=0)
        mean = _group_sum(o, g64_ref, True) * (1.0 / N_B)
        d = o - mean
        var = _group_sum(d * d, g64_ref, True) * (1.0 / N_B)
        y = d * lax.rsqrt(var + RWKV_LN_EPS) * lnw_ref[...] + lnb_ref[...] + bonus_ref[bi].astype(F32)
        o_ref[bi] = (y * g_ref[bi].astype(F32)).astype(BF16)


def _rwkv_scan(mix, logs, ln_w, ln_b, g64):
    B, S, _ = mix.shape
    W = W_B
    t = min(RWKV_T, S)
    nb = RWKV_NB if B % RWKV_NB == 0 else 1
    tok = lambda j: pl.BlockSpec((nb, t, W), lambda b, c: (b, c, j))
    m0 = _M_B // W
    return pl.pallas_call(
        _rwkv_kernel,
        grid=(B // nb, S // t),
        in_specs=[tok(m0 + j) for j in range(5)] + [tok(0), tok(m0 + 5), tok(m0 + 6)]
                 + [_const_spec((1, W)), _const_spec((1, W)), _const_spec(g64.shape)],
        out_specs=tok(0),
        out_shape=jax.ShapeDtypeStruct((B, S, W), BF16),
        scratch_shapes=[pltpu.VMEM((nb * H_B // 2, LANES, LANES), F32)],
        compiler_params=_cparams(2),
        name="rwkv7_scan",
    )(mix, mix, mix, mix, mix, logs, mix, mix, ln_w, ln_b, g64)


def _hgrn_kernel(q_ref, k_ref, v_ref, lf_ref, gate_ref, nw_ref, o_ref, state_ref):
    C = HGRN_C
    c = HGRN_SUB
    n_chunks = q_ref.shape[1] // C

    @pl.when(pl.program_id(1) == 0)
    def _():
        state_ref[...] = jnp.zeros_like(state_ref)

    lane_c = lax.broadcasted_iota(jnp.int32, (1, c), 1)
    row_c = lax.broadcasted_iota(jnp.int32, (c, 1), 0)

    units = []
    for n in range(n_chunks):
        rows = slice(n * C, (n + 1) * C)
        b = _cumsum_rows(lf_ref[0, rows, :], C) * math.log2(math.e)
        b_end = b[C - 1:C]
        q = q_ref[0, rows, :].astype(F32)
        k = k_ref[0, rows, :].astype(F32)
        v_all = v_ref[0, rows, :]
        q_in = (q * jnp.exp2(b)).astype(BF16)
        k_end = (k * jnp.exp2(b_end - b)).astype(BF16)
        p_end = jnp.exp2(b_end)
        for h in range(H_C):
            sl = slice(h * LANES, (h + 1) * LANES)
            units.append(dict(b=b[:, sl], q=q[:, sl], k=k[:, sl], v=v_all[:, sl], q_in=q_in[:, sl],
                              k_end=k_end[:, sl], p_end=p_end[:, sl]))

    for d in units:
        d["cross"] = []
        for i in range(1, C // c):
            r0 = i * c
            b_ref_ = d["b"][r0 - 1:r0]
            qs = (d["q"][r0:r0 + c] * jnp.exp2(d["b"][r0:r0 + c] - b_ref_)).astype(BF16)
            ks = (d["k"][0:r0] * jnp.exp2(b_ref_ - d["b"][0:r0])).astype(BF16)
            d["cross"].append(_dot_nt(qs, ks).astype(BF16))
    for d in units:
        d["diag"] = []
        for i in range(C // c):
            r0 = i * c
            b_i = d["b"][r0:r0 + c]
            q_i = d["q"][r0:r0 + c]
            k_i = d["k"][r0:r0 + c]
            sc = jnp.zeros((c, c), F32)
            for j in range(c):
                e = jnp.exp2(b_i - b_i[j:j + 1])
                col = jnp.sum(q_i * e * k_i[j:j + 1], axis=-1, keepdims=True)
                sc = jnp.where(lane_c == j, col, sc)
            d["diag"].append(jnp.where(lane_c <= row_c, sc, 0.0).astype(BF16))
    for d in units:
        blocks = []
        for i in range(C // c):
            r0 = i * c
            o_i = _dot(d["diag"][i], d["v"][r0:r0 + c])
            if i > 0:
                o_i = o_i + _dot(d["cross"][i - 1], d["v"][0:r0])
            blocks.append(o_i)
        d["intra"] = jnp.concatenate(blocks, axis=0)

    state = [state_ref[h] for h in range(H_C)]
    o_chunks = []
    for n in range(n_chunks):
        us = units[n * H_C:(n + 1) * H_C]
        o_heads = [_dot_nt(d["q_in"], st.astype(BF16)) + d["intra"] for d, st in zip(us, state)]
        state = [st * d["p_end"] + _dot_tn(d["v"], d["k_end"]) for d, st in zip(us, state)]
        o_chunks.append(jnp.concatenate(
            [o_h * lax.rsqrt(jnp.mean(o_h * o_h, axis=-1, keepdims=True) + EPS) * nw_ref[...] for o_h in o_heads],
            axis=1))
    for h in range(H_C):
        state_ref[h] = state[h]
    o = jnp.concatenate(o_chunks, axis=0)
    o_ref[0] = (o * gate_ref[0].astype(F32)).astype(BF16)


def _hgrn_scan(mix, logs, norm_w):
    B, S, _ = mix.shape
    W = W_C
    t = min(HGRN_T, S)
    tok = lambda j: pl.BlockSpec((1, t, W), lambda b, c: (b, c, j))
    m0 = _M_C // W
    return pl.pallas_call(
        _hgrn_kernel,
        grid=(B, S // t),
        in_specs=[tok(m0), tok(m0 + 1), tok(m0 + 2), tok(W_B // W), tok(m0 + 3), _const_spec((1, LANES))],
        out_specs=tok(0),
        out_shape=jax.ShapeDtypeStruct((B, S, W), BF16),
        scratch_shapes=[pltpu.VMEM((H_C, LANES, LANES), F32)],
        compiler_params=_cparams(2),
        name="hgrn2_scan",
    )(mix, mix, mix, logs, mix, norm_w)


def _merge_kernel(h_ref, mod_ref, oa_ref, ob_ref, oc_ref, gates_ref, wa_ref, wb_ref, wc_ref, wo_ref, o_ref):
    D = h_ref.shape[2]
    gates = gates_ref[0]
    z = (gates[:, 0:D].astype(F32) * _dot(oa_ref[0], wa_ref[...])
         + gates[:, D:2 * D].astype(F32) * _dot(ob_ref[0], wb_ref[...])
         + gates[:, 2 * D:3 * D].astype(F32) * _dot(oc_ref[0], wc_ref[...]))
    y = _dot(z.astype(BF16), wo_ref[...])
    o_ref[0] = h_ref[0] + mod_ref[0][5:6] * y


def _merge(h, mod, oa, ob, oc, mix, wa, wb, wc, wo):
    B, S, D = h.shape
    tm = min(MERGE_TM, S)
    tok = lambda width: pl.BlockSpec((1, tm, width), lambda b, s: (b, s, 0))
    return pl.pallas_call(
        _merge_kernel,
        grid=(B, S // tm),
        in_specs=[tok(D), pl.BlockSpec((1, N_MOD, D), lambda b, s: (b, 0, 0)),
                  tok(W_A), tok(W_B), tok(W_C), tok(3 * D),
                  _const_spec((W_A, D)), _const_spec((W_B, D)), _const_spec((W_C, D)), _const_spec((D, D))],
        out_specs=tok(D),
        out_shape=jax.ShapeDtypeStruct((B, S, D), F32),
        compiler_params=_cparams(2),
        name="merge",
    )(h, mod, oa, ob, oc, mix, wa, wb, wc, wo)


def _block_ones(width, group):
    i = jnp.arange(width) // group
    return (i[:, None] == i[None, :]).astype(BF16)


def kernel(x, c, positions, mod_w, mod_b, norm_w, ffn_w_gate, ffn_w_up, ffn_w_down, w_in, qk_norm_w, lambda_qk,
           subln_w, w_out_a, rwkv_mu, rwkv_w0, rwkv_w2, rwkv_a0, rwkv_a2, rwkv_g2, rwkv_k_k, rwkv_k_a, rwkv_r_k,
           rwkv_ln_w, rwkv_ln_b, w_out_b, hgrn_lower_bounds, hgrn_norm_w, w_out_c, w_out):
    B, S, D = x.shape
    L = mod_w.shape[0]
    lb = jnp.cumsum(jax.nn.softmax(hgrn_lower_bounds.astype(F32), axis=0), axis=0)
    lb = lb - lb[0]
    log_lb = jnp.log(lb)
    log1m_lb = jnp.log1p(-lb)

    mod = _modulation(c, mod_w, mod_b).reshape(L, B, N_MOD, D)
    cos, sin = _rope_tables(positions)
    g64 = _block_ones(2 * LANES, N_B)
    pad_cols = R_G_PAD - R_G
    zrow = lambda n: jnp.zeros((n, W_B), F32)

    h = x
    for l in range(L):
        wg = ffn_w_gate[l].astype(BF16)
        wu = ffn_w_up[l].astype(BF16)
        wd = ffn_w_down[l].astype(BF16)
        h = _ffn(h, mod[l], norm_w[l, 0][None], wg[0], wu[0], wd[0], 0)

        w_pad = jnp.concatenate([w_in[l][:, :P_A + P_B], jnp.zeros((D, pad_cols), F32),
                                 w_in[l][:, P_A + P_B:]], axis=1).astype(BF16)
        mu = jnp.concatenate([rwkv_mu[l], jnp.zeros((pad_cols,), F32)])[None]
        w2p = jnp.concatenate([rwkv_w2[l], zrow(R_A)], axis=0).astype(BF16)
        a2p = jnp.concatenate([zrow(R_W), rwkv_a2[l]], axis=0).astype(BF16)
        g2p = jnp.concatenate([rwkv_g2[l], zrow(pad_cols)], axis=0).astype(BF16)
        qkw = jnp.tile(qk_norm_w[l], (1, W_A // DH_A))
        mix, logs = _mixer_in(
            h, mod[l], norm_w[l, 1][None], w_pad, cos, sin, qkw, g64, mu,
            rwkv_w0[l][None], w2p, rwkv_a0[l][None], a2p, g2p, rwkv_k_k[l][None], rwkv_k_a[l][None],
            rwkv_r_k[l].reshape(1, W_B), log_lb[l][None], log1m_lb[l][None])
        oa = _diff_attention(mix, lambda_qk[l], subln_w[l][None], l)
        ob = _rwkv_scan(mix, logs, rwkv_ln_w[l][None], rwkv_ln_b[l][None], g64)
        oc = _hgrn_scan(mix, logs, hgrn_norm_w[l][None])
        h = _merge(h, mod[l], oa, ob, oc, mix, w_out_a[l].astype(BF16), w_out_b[l].astype(BF16),
                   w_out_c[l].astype(BF16), w_out[l].astype(BF16))

        h = _ffn(h, mod[l], norm_w[l, 2][None], wg[1], wu[1], wd[1], 6)
    return h
```

```python
import functools
import math

import jax
import jax.numpy as jnp
from jax import lax
from jax.experimental import pallas as pl
from jax.experimental.pallas import tpu as pltpu

F32 = jnp.float32
BF16 = jnp.bfloat16

H_A, DH_A = 4, 64
W_A = H_A * 2 * DH_A
ROPE_THETA = 10000.0
H_B, N_B = 8, 64
W_B = H_B * N_B
R_W, R_A, R_G = 64, 64, 160
RWKV_LN_EPS = 64e-5
H_C, DK_C = 4, 128
W_C = H_C * DK_C
N_MOD = 9
EPS = 1e-6
P_A = 3 * W_A
P_B = 3 * W_B + R_W + R_A + R_G
P_C = 4 * W_C

LANES = 128
R_G_PAD = 256
P_B_PAD = 3 * W_B + R_W + R_A + R_G_PAD
VMEM_LIMIT = 56 * 1024 * 1024

FFN_TM = 512
MIX_TM = 256
ATT_T = 512
RWKV_C = 64
RWKV_T = 128
RWKV_NB = 2
HGRN_C = 64
HGRN_T = 256
HGRN_SUB = 8
MERGE_TM = 512


def _cparams(n_axes):
    return pltpu.CompilerParams(dimension_semantics=("arbitrary",) * n_axes,
                                vmem_limit_bytes=VMEM_LIMIT)


def _const_spec(shape):
    nd = len(shape)
    return pl.BlockSpec(shape, lambda *_: (0,) * nd, pipeline_mode=pl.Buffered(1))


def _sigmoid(x):
    return 1.0 / (1.0 + jnp.exp(-x))


def _silu(x):
    return x * _sigmoid(x)


def _log1p_exp_neg_abs(x):
    return jnp.log(1.0 + jnp.exp(-jnp.abs(x)))


def _dot(a, b):
    return jnp.dot(a, b, preferred_element_type=F32)


def _dot_nt(a, b):
    return lax.dot_general(a, b, (((1,), (1,)), ((), ())), preferred_element_type=F32)


def _dot_tn(a, b):
    return lax.dot_general(a, b, (((0,), (0,)), ((), ())), preferred_element_type=F32)


def _group_sum(x, g_ref, split):
    g = g_ref[...]
    w = g.shape[0]

    def halves(xb):
        return jnp.concatenate([_dot(xb[:, i * w:(i + 1) * w], g) for i in range(x.shape[1] // w)], axis=1)

    hi = x.astype(BF16)
    if not split:
        return halves(hi)
    return halves(hi) + halves((x - hi.astype(F32)).astype(BF16))


def _norm_mod(x, nw, shift, scale):
    ms = jnp.mean(x * x, axis=-1, keepdims=True)
    return (x * lax.rsqrt(ms + EPS) * nw) * (1.0 + scale) + shift


def _cumsum_rows(x, n):
    row = lax.broadcasted_iota(jnp.int32, (n, 1), 0)
    s = 1
    while s < n:
        x = x + jnp.where(row >= s, pltpu.roll(x, s, 0), 0.0)
        s *= 2
    return x


def _mod_kernel(c_ref, w_ref, b_ref, o_ref):
    c = c_ref[...]
    cond = _silu(c)
    o_ref[0] = jnp.dot(cond, w_ref[0], preferred_element_type=F32,
                       precision=lax.Precision.HIGHEST) + b_ref[0]


def _modulation(c, mod_w, mod_b):
    L, D, N = mod_w.shape
    B = c.shape[0]
    tn = 1024
    return pl.pallas_call(
        _mod_kernel,
        grid=(L, N // tn),
        in_specs=[pl.BlockSpec((B, D), lambda l, j: (0, 0)),
                  pl.BlockSpec((1, D, tn), lambda l, j: (l, 0, j)),
                  pl.BlockSpec((1, 1, tn), lambda l, j: (l, 0, j))],
        out_specs=pl.BlockSpec((1, B, tn), lambda l, j: (l, 0, j)),
        out_shape=jax.ShapeDtypeStruct((L, B, N), F32),
        compiler_params=_cparams(2),
        name="modulation",
    )(c, mod_w, mod_b.reshape(L, 1, N))


def _rope_kernel(pos_ref, inv_ref, sign_ref, cos_ref, sin_ref):
    ang = pos_ref[0] * inv_ref[...]
    cos_ref[0] = jnp.cos(ang)
    sin_ref[0] = jnp.sin(ang) * sign_ref[...]


def _rope_tables(positions):
    B, S = positions.shape
    ts = 512 if S % 512 == 0 else S
    half = DH_A // 2
    inv = ROPE_THETA ** (-jnp.arange(0, DH_A, 2, dtype=F32) / DH_A)
    inv128 = jnp.tile(inv, LANES // half)[None, :]
    sign128 = jnp.tile(jnp.concatenate([-jnp.ones(half, F32), jnp.ones(half, F32)]), LANES // DH_A)[None, :]
    pos = positions.astype(F32)[..., None]
    return pl.pallas_call(
        _rope_kernel,
        grid=(B, S // ts),
        in_specs=[pl.BlockSpec((1, ts, 1), lambda b, s: (b, s, 0)),
                  pl.BlockSpec((1, LANES), lambda b, s: (0, 0)),
                  pl.BlockSpec((1, LANES), lambda b, s: (0, 0))],
        out_specs=[pl.BlockSpec((1, ts, LANES), lambda b, s: (b, s, 0))] * 2,
        out_shape=[jax.ShapeDtypeStruct((B, S, LANES), F32)] * 2,
        compiler_params=_cparams(2),
        name="rope_tables",
    )(pos, inv128, sign128)


def _ffn_kernel(chunks, shift_i, h_ref, mod_ref, nw_ref, wg_ref, wu_ref, wd_ref, o_ref, act_ref):
    x = h_ref[0]
    m = mod_ref[0]
    u = _norm_mod(x, nw_ref[...], m[shift_i:shift_i + 1], m[shift_i + 1:shift_i + 2]).astype(BF16)
    for lo, hi in chunks:
        g = _dot(u, wg_ref[:, lo:hi])
        up = _dot(u, wu_ref[:, lo:hi])
        act_ref[:, lo:hi] = (_silu(g) * up).astype(BF16)
    y = _dot(act_ref[...], wd_ref[...])
    o_ref[0] = x + (0.5 * m[shift_i + 2:shift_i + 3]) * y


def _ffn(h, mod, nw, wg, wu, wd, shift_i):
    B, S, D = h.shape
    F = wg.shape[1]
    tm = min(FFN_TM, S)
    chunks, lo = [], 0
    while lo < F:
        hi = min(lo + 512, F)
        chunks.append((lo, hi))
        lo = hi
    return pl.pallas_call(
        functools.partial(_ffn_kernel, tuple(chunks), shift_i),
        grid=(B, S // tm),
        in_specs=[pl.BlockSpec((1, tm, D), lambda b, s: (b, s, 0)),
                  pl.BlockSpec((1, N_MOD, D), lambda b, s: (b, 0, 0)),
                  _const_spec((1, D)), _const_spec((D, F)), _const_spec((D, F)), _const_spec((F, D))],
        out_specs=pl.BlockSpec((1, tm, D), lambda b, s: (b, s, 0)),
        out_shape=jax.ShapeDtypeStruct((B, S, D), F32),
        scratch_shapes=[pltpu.VMEM((tm, F), BF16)],
        compiler_params=_cparams(2),
        name="ffn",
    )(h, mod, nw, wg, wu, wd)


_O_QA, _O_KA, _O_VA = 0, W_A, 2 * W_A
_O_B = P_A
_O_C = P_A + P_B_PAD
_O_G = _O_C + P_C


_M_GATES = 0
_M_A = 3 * 1024
_M_B = _M_A + P_A
_M_C = _M_B + 7 * W_B
_M_TOTAL = _M_C + P_C


def _mixin_kernel(h_ref, mod_ref, nw_ref, w_ref, cos_ref, sin_ref, qkw_ref, g64_ref, mu_ref,
                  w0_ref, w2_ref, a0_ref, a2_ref, g2_ref, kk_ref, ka_ref, rk_ref, llb_ref, l1lb_ref,
                  mix_o, logs_o, prev_ref):
    tm = h_ref.shape[1]
    D = h_ref.shape[2]
    x = h_ref[0]
    m = mod_ref[0]
    u = _norm_mod(x, nw_ref[...], m[3:4], m[4:5]).astype(BF16)

    def put(off, val):
        mix_o[0, :, off:off + val.shape[1]] = val.astype(BF16)

    def qk_prep(p, wrow, scale):
        cos4 = jnp.concatenate([cos_ref[0]] * (W_A // LANES), axis=1)
        sin4 = jnp.concatenate([sin_ref[0]] * (W_A // LANES), axis=1)
        lane = lax.broadcasted_iota(jnp.int32, (1, W_A), 1)
        first_half = (lane % DH_A) < (DH_A // 2)
        ss = _group_sum(p * p, g64_ref, False)
        n = p * lax.rsqrt(ss * (1.0 / DH_A) + EPS) * wrow
        rot = jnp.where(first_half, pltpu.roll(n, W_A - DH_A // 2, 1), pltpu.roll(n, DH_A // 2, 1))
        return (n * cos4 + rot * sin4) * scale

    def use_qa(p):
        put(_M_A, qk_prep(p, qkw_ref[0:1], DH_A ** -0.5 * math.log2(math.e)))

    def use_ka(p):
        put(_M_A + W_A, qk_prep(p, qkw_ref[1:2], 1.0))

    def use_va(p):
        put(_M_A + 2 * W_A, p)

    rw = {}

    def use_rwkv_shift(pb):
        @pl.when(pl.program_id(1) == 0)
        def _():
            prev_ref[...] = jnp.zeros_like(prev_ref)

        row = lax.broadcasted_iota(jnp.int32, (tm, 1), 0)
        shifted = jnp.where(row == 0, prev_ref[...], pltpu.roll(pb, 1, 0))
        prev_ref[...] = pb[tm - 1:tm]
        pb = pb + (shifted - pb) * mu_ref[...]
        rw["r"] = pb[:, 0:W_B]
        rw["k"] = pb[:, W_B:2 * W_B]
        rw["v"] = pb[:, 2 * W_B:3 * W_B]
        rw["xwa"] = pb[:, 3 * W_B:3 * W_B + R_W + R_A]
        rw["xg"] = pb[:, 3 * W_B + R_W + R_A:]
        put(_M_B, rw["r"])
        put(_M_B + 2 * W_B, rw["v"])

    def use_rwkv_loras():
        xwa = rw["xwa"]
        z = w0_ref[...] + _dot(jnp.tanh(xwa).astype(BF16), w2_ref[...])
        logs_o[0, :, 0:W_B] = (-math.exp(-0.5)) * _sigmoid(z)
        rw["a"] = _sigmoid(a0_ref[...] + _dot(xwa.astype(BF16), a2_ref[...]))
        put(_M_B + 5 * W_B, _dot(_sigmoid(rw["xg"]).astype(BF16), g2_ref[...]))

    def use_rwkv_keys():
        k, a = rw["k"], rw["a"]
        kk = k * kk_ref[...]
        kkn = kk / jnp.maximum(jnp.sqrt(_group_sum(kk * kk, g64_ref, False)), 1e-12)
        rw["k2"] = k * (1.0 + (a - 1.0) * ka_ref[...])
        put(_M_B + W_B, rw["k2"])
        put(_M_B + 3 * W_B, kkn)
        put(_M_B + 4 * W_B, kkn * a)

    def use_rwkv_bonus():
        put(_M_B + 6 * W_B, _group_sum(rw["r"] * rw["k2"] * rk_ref[...], g64_ref, False) * rw["v"])

    def use_qc(p):
        put(_M_C, _silu(p))

    def use_fz(fz):
        log_sig = jnp.minimum(fz, 0.0) - _log1p_exp_neg_abs(fz)
        t1 = llb_ref[...]
        t2 = l1lb_ref[...] + log_sig
        logf = jnp.maximum(t1, t2) + _log1p_exp_neg_abs(t1 - t2)
        logs_o[0, :, W_B:W_B + W_C] = logf
        put(_M_C + W_C, 1.0 - jnp.exp(logf))

    def use_ic(p):
        put(_M_C + 2 * W_C, p)

    def use_gc(p):
        put(_M_C + 3 * W_C, _silu(p))

    def use_gates(j):
        return lambda p: put(_M_GATES + j * D, _sigmoid(p))

    seg = {"qa": (_O_QA, W_A), "ka": (_O_KA, W_A), "va": (_O_VA, W_A), "b": (_O_B, P_B_PAD),
           "qc": (_O_C, W_C), "fz": (_O_C + W_C, W_C), "ic": (_O_C + 2 * W_C, W_C), "gc": (_O_C + 3 * W_C, W_C),
           "g0": (_O_G, D), "g1": (_O_G + D, D), "g2": (_O_G + 2 * D, D)}
    p = {}

    def proj(name):
        lo, width = seg[name]
        p[name] = _dot(u, w_ref[:, lo:lo + width])

    proj("qa")
    proj("ka")
    use_qa(p.pop("qa"))
    proj("va")
    use_ka(p.pop("ka"))
    proj("b")
    use_va(p.pop("va"))
    proj("qc")
    use_rwkv_shift(p.pop("b"))
    proj("fz")
    use_rwkv_loras()
    proj("ic")
    use_rwkv_keys()
    proj("gc")
    use_rwkv_bonus()
    use_qc(p.pop("qc"))
    proj("g0")
    use_fz(p.pop("fz"))
    use_ic(p.pop("ic"))
    proj("g1")
    use_gc(p.pop("gc"))
    use_gates(0)(p.pop("g0"))
    proj("g2")
    use_gates(1)(p.pop("g1"))
    use_gates(2)(p.pop("g2"))


def _mixer_in(h, mod, nw, w_pad, cos, sin, qkw, g64, mu, w0, w2p, a0, a2p, g2p, k_k, k_a, r_k, llb, l1lb):
    B, S, D = h.shape
    tm = min(MIX_TM, S)
    PT = w_pad.shape[1]
    tok = lambda width: pl.BlockSpec((1, tm, width), lambda b, s: (b, s, 0))
    return pl.pallas_call(
        _mixin_kernel,
        grid=(B, S // tm),
        in_specs=[tok(D), pl.BlockSpec((1, N_MOD, D), lambda b, s: (b, 0, 0)), _const_spec((1, D)),
                  _const_spec((D, PT)), tok(LANES), tok(LANES), _const_spec((2, W_A)), _const_spec(g64.shape),
                  _const_spec((1, P_B_PAD)), _const_spec((1, W_B)), _const_spec((R_W + R_A, W_B)),
                  _const_spec((1, W_B)), _const_spec((R_W + R_A, W_B)), _const_spec((R_G_PAD, W_B)),
                  _const_spec((1, W_B)), _const_spec((1, W_B)), _const_spec((1, W_B)),
                  _const_spec((1, W_C)), _const_spec((1, W_C))],
        out_specs=[tok(_M_TOTAL), tok(W_B + W_C)],
        out_shape=[jax.ShapeDtypeStruct((B, S, _M_TOTAL), BF16), jax.ShapeDtypeStruct((B, S, W_B + W_C), F32)],
        scratch_shapes=[pltpu.VMEM((1, P_B_PAD), F32)],
        compiler_params=_cparams(2),
        name="mixer_in",
    )(h, mod, nw, w_pad, cos, sin, qkw, g64, mu, w0, w2p, a0, a2p, g2p, k_k, k_a, r_k, llb, l1lb)


def _attn_kernel(lam_init, t, q_ref, k_ref, v_ref, lq_ref, sw_ref, o_ref, qs_ref, m_ref, l_ref, acc_ref):
    n = q_ref.shape[1] // t
    lane = lax.broadcasted_iota(jnp.int32, (1, LANES), 1)
    for i in range(n):
        q = q_ref[0, i * t:(i + 1) * t, :]
        zero = jnp.zeros_like(q)
        qs_ref[i, 0:t] = jnp.where(lane < DH_A, q, zero)
        qs_ref[i, t:2 * t] = jnp.where(lane < DH_A, zero, q)
    m_ref[...] = jnp.full_like(m_ref, -jnp.inf)
    l_ref[...] = jnp.zeros_like(l_ref)
    acc_ref[...] = jnp.zeros_like(acc_ref)
    lq = lq_ref[...]
    lam = (jnp.exp(jnp.sum(lq[0:1] * lq[1:2], axis=-1, keepdims=True))
           - jnp.exp(jnp.sum(lq[2:3] * lq[3:4], axis=-1, keepdims=True)) + lam_init)

    def scores(i, j):
        return _dot_nt(qs_ref[i], k_ref[0, j * t:(j + 1) * t, :])

    def update(i, j, s):
        if i == j:
            r_i = lax.broadcasted_iota(jnp.int32, (2 * t, t), 0) % t
            c_i = lax.broadcasted_iota(jnp.int32, (2 * t, t), 1)
            s = jnp.where(c_i <= r_i, s, -jnp.inf)
        m_old = m_ref[i]
        m_new = jnp.maximum(m_old, jnp.max(s, axis=-1, keepdims=True))
        alpha = jnp.exp2(m_old - m_new)
        p = jnp.exp2(s - jnp.concatenate([m_new] * (t // LANES), axis=1))
        l_ref[i] = alpha * l_ref[i] + jnp.sum(p, axis=-1, keepdims=True)
        acc_ref[i] = alpha * acc_ref[i] + _dot(p.astype(BF16), v_ref[0, j * t:(j + 1) * t, :])
        m_ref[i] = m_new
        if i == j:
            o = acc_ref[i] / l_ref[i]
            o = o[0:t] - lam * o[t:2 * t]
            ms = jnp.mean(o * o, axis=-1, keepdims=True)
            o_ref[0, i * t:(i + 1) * t, :] = (o * lax.rsqrt(ms + EPS) * sw_ref[...] * (1.0 - lam_init)).astype(BF16)

    chains = [[], []]
    for a in range((n + 1) // 2):
        tiles = [a] if n - 1 - a == a else [a, n - 1 - a]
        shorter = chains[0] if len(chains[0]) <= len(chains[1]) else chains[1]
        shorter += [(i, j) for i in tiles for j in range(i + 1)]
    chain_a, chain_b = chains
    order = []
    for x in range(max(len(chain_a), len(chain_b))):
        order += chain_a[x:x + 1] + chain_b[x:x + 1]
    s_next = scores(*order[0])
    for x, (i, j) in enumerate(order):
        s_cur = s_next
        if x + 1 < len(order):
            s_next = scores(*order[x + 1])
        update(i, j, s_cur)


def _diff_attention(mix, lambda_qk, subln_w, layer_idx):
    B, S, _ = mix.shape
    t = min(ATT_T, S)
    n = S // t
    lam_init = 0.8 - 0.6 * math.exp(-0.3 * layer_idx)
    blk = lambda off: off // LANES
    spec = lambda off: pl.BlockSpec((1, S, LANES), lambda b, h: (b, 0, blk(off) + h))
    return pl.pallas_call(
        functools.partial(_attn_kernel, lam_init, t),
        grid=(B, H_A),
        in_specs=[spec(_M_A), spec(_M_A + W_A), spec(_M_A + 2 * W_A),
                  pl.BlockSpec((4, DH_A), lambda b, h: (0, 0)),
                  pl.BlockSpec((1, 2 * DH_A), lambda b, h: (0, 0))],
        out_specs=pl.BlockSpec((1, S, LANES), lambda b, h: (b, 0, h)),
        out_shape=jax.ShapeDtypeStruct((B, S, W_A), BF16),
        scratch_shapes=[pltpu.VMEM((n, 2 * t, LANES), BF16), pltpu.VMEM((n, 2 * t, LANES), F32),
                        pltpu.VMEM((n, 2 * t, LANES), F32), pltpu.VMEM((n, 2 * t, LANES), F32)],
        compiler_params=_cparams(2),
        name="diff_attention",
    )(mix, mix, mix, lambda_qk, subln_w)


def _rwkv_kernel(r_ref, k_ref, v_ref, kk_ref, beta_ref, lw_ref, g_ref, bonus_ref, lnw_ref, lnb_ref, g64_ref,
                 o_ref, state_ref):
    C = RWKV_C
    C2 = 2 * C
    n_chunks = r_ref.shape[1] // C

    @pl.when(pl.program_id(1) == 0)
    def _():
        state_ref[...] = jnp.zeros_like(state_ref)

    lane = lax.broadcasted_iota(jnp.int32, (1, LANES), 1)
    head0 = lane < N_B
    ri = lax.broadcasted_iota(jnp.int32, (C, 1), 0)
    ci = lax.broadcasted_iota(jnp.int32, (1, C2), 1) % C
    strict = ci < ri
    incl = ci <= ri
    same_state = (lax.broadcasted_iota(jnp.int32, (LANES, 1), 0) < N_B) == head0

    def stack(x):
        z = jnp.zeros_like(x)
        return jnp.concatenate([jnp.where(head0, x, z), jnp.where(head0, z, x)], axis=0)

    n_pairs = H_B // 2
    n_seq = r_ref.shape[0]
    chains = []
    for bi, n in [(bi, n) for n in range(n_chunks) for bi in range(n_seq)]:
        rows = slice(n * C, (n + 1) * C)
        lw = lw_ref[bi, rows, :]
        cum = _cumsum_rows(lw, C)
        c_last = cum[C - 1:C]
        p_inv = jnp.exp(-cum)
        p_end = jnp.exp(c_last - cum)
        p_last = jnp.exp(c_last)
        k = k_ref[bi, rows, :].astype(F32)
        beta = beta_ref[bi, rows, :].astype(F32)
        v_all = v_ref[bi, rows, :]
        rt = (r_ref[bi, rows, :].astype(F32) * jnp.exp(cum)).astype(BF16)
        kt = (kk_ref[bi, rows, :].astype(F32) * jnp.exp(cum - lw)).astype(BF16)
        kh = (k * p_inv).astype(BF16)
        bh = (beta * p_inv).astype(BF16)
        ke = (k * p_end).astype(BF16)
        be = (beta * p_end).astype(BF16)
        for p in range(n_pairs):
            sl = slice(p * LANES, (p + 1) * LANES)
            v = v_all[:, sl]
            chains.append(dict(
                ktrt=jnp.concatenate([kt[:, sl], rt[:, sl]], axis=0),
                rhs4=jnp.concatenate([stack(kh[:, sl]), stack(bh[:, sl])], axis=0),
                v=v, v_s=stack(v),
                kebe=jnp.concatenate([ke[:, sl], -be[:, sl]], axis=0),
                p_last=p_last[:, sl]))

    for d in chains:
        a_all = _dot_nt(d["ktrt"], d.pop("rhs4"))
        d["a_kk"] = jnp.where(strict, a_all[0:C, 0:C2], 0.0).astype(BF16)
        d["a_o"] = jnp.concatenate([jnp.where(incl, a_all[C:, 0:C2], 0.0),
                                    -jnp.where(incl, a_all[C:, C2:], 0.0)], axis=1).astype(BF16)
        d["xm"] = -jnp.where(strict, a_all[0:C, C2:], 0.0)
    for d in chains:
        pw = d["xm"].astype(BF16)
        d["pf"] = _dot(pw, stack(pw))
    s = 2
    while s < C:
        for d in chains:
            pf = d["pf"]
            pw = pf.astype(BF16)
            xs = stack(d["xm"].astype(BF16))
            if 2 * s >= C:
                d["xm"] = d["xm"] + pf + _dot(pw, xs)
            else:
                both = _dot(pw, jnp.concatenate([stack(pw), xs], axis=1))
                d["xm"] = d["xm"] + pf + both[:, C2:]
                d["pf"] = both[:, 0:C2]
        s *= 2
    for d in chains:
        d["av"] = _dot(d.pop("a_kk"), d["v_s"])
        d["xm"] = d["xm"].astype(BF16)

    n_state = n_seq * n_pairs
    state = [state_ref[i] for i in range(n_state)]
    o_chunks = []
    for n in range(n_chunks):
        cs = chains[n * n_state:(n + 1) * n_state]
        w1r1 = [_dot_nt(d["ktrt"], st.astype(BF16)) for d, st in zip(cs, state)]
        rhs = [w[0:C] + d["av"] for d, w in zip(cs, w1r1)]
        u_b = [(x + _dot(d["xm"], stack(x.astype(BF16)))).astype(BF16) for d, x in zip(cs, rhs)]
        outs = [w[C:C2] + _dot(d["a_o"], jnp.concatenate([d["v_s"], stack(u)], axis=0))
                for d, w, u in zip(cs, w1r1, u_b)]
        upd = [_dot_tn(jnp.concatenate([d["v"], u], axis=0), d["kebe"]) for d, u in zip(cs, u_b)]
        state = [st * d["p_last"] + jnp.where(same_state, x, 0.0) for d, st, x in zip(cs, state, upd)]
        o_chunks.append([jnp.concatenate(outs[bi * n_pairs:(bi + 1) * n_pairs], axis=1) for bi in range(n_seq)])
    for i in range(n_state):
        state_ref[i] = state[i]

    for bi in range(n_seq):
        o = jnp.concatenate([oc[bi] for oc in o_chunks], axis=0)
        mean = _group_sum(o, g64_ref, True) * (1.0 / N_B)
        d = o - mean
        var = _group_sum(d * d, g64_ref, True) * (1.0 / N_B)
        y = d * lax.rsqrt(var + RWKV_LN_EPS) * lnw_ref[...] + lnb_ref[...] + bonus_ref[bi].astype(F32)
        o_ref[bi] = (y * g_ref[bi].astype(F32)).astype(BF16)


def _rwkv_scan(mix, logs, ln_w, ln_b, g64):
    B, S, _ = mix.shape
    W = W_B
    t = min(RWKV_T, S)
    nb = RWKV_NB if B % RWKV_NB == 0 else 1
    tok = lambda j: pl.BlockSpec((nb, t, W), lambda b, c: (b, c, j))
    m0 = _M_B // W
    return pl.pallas_call(
        _rwkv_kernel,
        grid=(B // nb, S // t),
        in_specs=[tok(m0 + j) for j in range(5)] + [tok(0), tok(m0 + 5), tok(m0 + 6)]
                 + [_const_spec((1, W)), _const_spec((1, W)), _const_spec(g64.shape)],
        out_specs=tok(0),
        out_shape=jax.ShapeDtypeStruct((B, S, W), BF16),
        scratch_shapes=[pltpu.VMEM((nb * H_B // 2, LANES, LANES), F32)],
        compiler_params=_cparams(2),
        name="rwkv7_scan",
    )(mix, mix, mix, mix, mix, logs, mix, mix, ln_w, ln_b, g64)


def _hgrn_kernel(q_ref, k_ref, v_ref, lf_ref, gate_ref, nw_ref, o_ref, state_ref):
    C = HGRN_C
    c = HGRN_SUB
    n_chunks = q_ref.shape[1] // C

    @pl.when(pl.program_id(1) == 0)
    def _():
        state_ref[...] = jnp.zeros_like(state_ref)

    lane_c = lax.broadcasted_iota(jnp.int32, (1, c), 1)
    row_c = lax.broadcasted_iota(jnp.int32, (c, 1), 0)

    units = []
    for n in range(n_chunks):
        rows = slice(n * C, (n + 1) * C)
        b = _cumsum_rows(lf_ref[0, rows, :], C) * math.log2(math.e)
        b_end = b[C - 1:C]
        q = q_ref[0, rows, :].astype(F32)
        k = k_ref[0, rows, :].astype(F32)
        v_all = v_ref[0, rows, :]
        q_in = (q * jnp.exp2(b)).astype(BF16)
        k_end = (k * jnp.exp2(b_end - b)).astype(BF16)
        p_end = jnp.exp2(b_end)
        for h in range(H_C):
            sl = slice(h * LANES, (h + 1) * LANES)
            units.append(dict(b=b[:, sl], q=q[:, sl], k=k[:, sl], v=v_all[:, sl], q_in=q_in[:, sl],
                              k_end=k_end[:, sl], p_end=p_end[:, sl]))

    for d in units:
        d["cross"] = []
        for i in range(1, C // c):
            r0 = i * c
            b_ref_ = d["b"][r0 - 1:r0]
            qs = (d["q"][r0:r0 + c] * jnp.exp2(d["b"][r0:r0 + c] - b_ref_)).astype(BF16)
            ks = (d["k"][0:r0] * jnp.exp2(b_ref_ - d["b"][0:r0])).astype(BF16)
            d["cross"].append(_dot_nt(qs, ks).astype(BF16))
    for d in units:
        d["diag"] = []
        for i in range(C // c):
            r0 = i * c
            b_i = d["b"][r0:r0 + c]
            q_i = d["q"][r0:r0 + c]
            k_i = d["k"][r0:r0 + c]
            sc = jnp.zeros((c, c), F32)
            for j in range(c):
                e = jnp.exp2(b_i - b_i[j:j + 1])
                col = jnp.sum(q_i * e * k_i[j:j + 1], axis=-1, keepdims=True)
                sc = jnp.where(lane_c == j, col, sc)
            d["diag"].append(jnp.where(lane_c <= row_c, sc, 0.0).astype(BF16))
    for d in units:
        blocks = []
        for i in range(C // c):
            r0 = i * c
            o_i = _dot(d["diag"][i], d["v"][r0:r0 + c])
            if i > 0:
                o_i = o_i + _dot(d["cross"][i - 1], d["v"][0:r0])
            blocks.append(o_i)
        d["intra"] = jnp.concatenate(blocks, axis=0)

    state = [state_ref[h] for h in range(H_C)]
    o_chunks = []
    for n in range(n_chunks):
        us = units[n * H_C:(n + 1) * H_C]
        o_heads = [_dot_nt(d["q_in"], st.astype(BF16)) + d["intra"] for d, st in zip(us, state)]
        state = [st * d["p_end"] + _dot_tn(d["v"], d["k_end"]) for d, st in zip(us, state)]
        o_chunks.append(jnp.concatenate(
            [o_h * lax.rsqrt(jnp.mean(o_h * o_h, axis=-1, keepdims=True) + EPS) * nw_ref[...] for o_h in o_heads],
            axis=1))
    for h in range(H_C):
        state_ref[h] = state[h]
    o = jnp.concatenate(o_chunks, axis=0)
    o_ref[0] = (o * gate_ref[0].astype(F32)).astype(BF16)


def _hgrn_scan(mix, logs, norm_w):
    B, S, _ = mix.shape
    W = W_C
    t = min(HGRN_T, S)
    tok = lambda j: pl.BlockSpec((1, t, W), lambda b, c: (b, c, j))
    m0 = _M_C // W
    return pl.pallas_call(
        _hgrn_kernel,
        grid=(B, S // t),
        in_specs=[tok(m0), tok(m0 + 1), tok(m0 + 2), tok(W_B // W), tok(m0 + 3), _const_spec((1, LANES))],
        out_specs=tok(0),
        out_shape=jax.ShapeDtypeStruct((B, S, W), BF16),
        scratch_shapes=[pltpu.VMEM((H_C, LANES, LANES), F32)],
        compiler_params=_cparams(2),
        name="hgrn2_scan",
    )(mix, mix, mix, logs, mix, norm_w)


def _merge_kernel(h_ref, mod_ref, oa_ref, ob_ref, oc_ref, gates_ref, wa_ref, wb_ref, wc_ref, wo_ref, o_ref):
    D = h_ref.shape[2]
    gates = gates_ref[0]
    z = (gates[:, 0:D].astype(F32) * _dot(oa_ref[0], wa_ref[...])
         + gates[:, D:2 * D].astype(F32) * _dot(ob_ref[0], wb_ref[...])
         + gates[:, 2 * D:3 * D].astype(F32) * _dot(oc_ref[0], wc_ref[...]))
    y = _dot(z.astype(BF16), wo_ref[...])
    o_ref[0] = h_ref[0] + mod_ref[0][5:6] * y


def _merge(h, mod, oa, ob, oc, mix, wa, wb, wc, wo):
    B, S, D = h.shape
    tm = min(MERGE_TM, S)
    tok = lambda width: pl.BlockSpec((1, tm, width), lambda b, s: (b, s, 0))
    return pl.pallas_call(
        _merge_kernel,
        grid=(B, S // tm),
        in_specs=[tok(D), pl.BlockSpec((1, N_MOD, D), lambda b, s: (b, 0, 0)),
                  tok(W_A), tok(W_B), tok(W_C), tok(3 * D),
                  _const_spec((W_A, D)), _const_spec((W_B, D)), _const_spec((W_C, D)), _const_spec((D, D))],
        out_specs=tok(D),
        out_shape=jax.ShapeDtypeStruct((B, S, D), F32),
        compiler_params=_cparams(2),
        name="merge",
    )(h, mod, oa, ob, oc, mix, wa, wb, wc, wo)


def _block_ones(width, group):
    i = jnp.arange(width) // group
    return (i[:, None] == i[None, :]).astype(BF16)


def kernel(x, c, positions, mod_w, mod_b, norm_w, ffn_w_gate, ffn_w_up, ffn_w_down, w_in, qk_norm_w, lambda_qk,
           subln_w, w_out_a, rwkv_mu, rwkv_w0, rwkv_w2, rwkv_a0, rwkv_a2, rwkv_g2, rwkv_k_k, rwkv_k_a, rwkv_r_k,
           rwkv_ln_w, rwkv_ln_b, w_out_b, hgrn_lower_bounds, hgrn_norm_w, w_out_c, w_out):
    B, S, D = x.shape
    L = mod_w.shape[0]
    lb = jnp.cumsum(jax.nn.softmax(hgrn_lower_bounds.astype(F32), axis=0), axis=0)
    lb = lb - lb[0]
    log_lb = jnp.log(lb)
    log1m_lb = jnp.log1p(-lb)

    mod = _modulation(c, mod_w, mod_b).reshape(L, B, N_MOD, D)
    cos, sin = _rope_tables(positions)
    g64 = _block_ones(2 * LANES, N_B)
    pad_cols = R_G_PAD - R_G
    zrow = lambda n: jnp.zeros((n, W_B), F32)

    h = x
    for l in range(L):
        wg = ffn_w_gate[l].astype(BF16)
        wu = ffn_w_up[l].astype(BF16)
        wd = ffn_w_down[l].astype(BF16)
        h = _ffn(h, mod[l], norm_w[l, 0][None], wg[0], wu[0], wd[0], 0)

        w_pad = jnp.concatenate([w_in[l][:, :P_A + P_B], jnp.zeros((D, pad_cols), F32),
                                 w_in[l][:, P_A + P_B:]], axis=1).astype(BF16)
        mu = jnp.concatenate([rwkv_mu[l], jnp.zeros((pad_cols,), F32)])[None]
        w2p = jnp.concatenate([rwkv_w2[l], zrow(R_A)], axis=0).astype(BF16)
        a2p = jnp.concatenate([zrow(R_W), rwkv_a2[l]], axis=0).astype(BF16)
        g2p = jnp.concatenate([rwkv_g2[l], zrow(pad_cols)], axis=0).astype(BF16)
        qkw = jnp.tile(qk_norm_w[l], (1, W_A // DH_A))
        mix, logs = _mixer_in(
            h, mod[l], norm_w[l, 1][None], w_pad, cos, sin, qkw, g64, mu,
            rwkv_w0[l][None], w2p, rwkv_a0[l][None], a2p, g2p, rwkv_k_k[l][None], rwkv_k_a[l][None],
            rwkv_r_k[l].reshape(1, W_B), log_lb[l][None], log1m_lb[l][None])
        oa = _diff_attention(mix, lambda_qk[l], subln_w[l][None], l)
        ob = _rwkv_scan(mix, logs, rwkv_ln_w[l][None], rwkv_ln_b[l][None], g64)
        oc = _hgrn_scan(mix, logs, hgrn_norm_w[l][None])
        h = _merge(h, mod[l], oa, ob, oc, mix, w_out_a[l].astype(BF16), w_out_b[l].astype(BF16),
                   w_out_c[l].astype(BF16), w_out[l].astype(BF16))

        h = _ffn(h, mod[l], norm_w[l, 2][None], wg[1], wu[1], wd[1], 6)
    return h
```

```python
import functools
import math

import jax
import jax.numpy as jnp
from jax import lax
from jax.experimental import pallas as pl
from jax.experimental.pallas import tpu as pltpu

F32 = jnp.float32
BF16 = jnp.bfloat16

H_A, DH_A = 4, 64
W_A = H_A * 2 * DH_A
ROPE_THETA = 10000.0
H_B, N_B = 8, 64
W_B = H_B * N_B
R_W, R_A, R_G = 64, 64, 160
RWKV_LN_EPS = 64e-5
H_C, DK_C = 4, 128
W_C = H_C * DK_C
N_MOD = 9
EPS = 1e-6
P_A = 3 * W_A
P_B = 3 * W_B + R_W + R_A + R_G
P_C = 4 * W_C

LANES = 128
R_G_PAD = 256
P_B_PAD = 3 * W_B + R_W + R_A + R_G_PAD
VMEM_LIMIT = 56 * 1024 * 1024

FFN_TM = 512
MIX_TM = 256
ATT_T = 512
RWKV_C = 64
RWKV_NB = 4
HGRN_C = 64
HGRN_T = 256
HGRN_SUB = 8
MERGE_TM = 512


def _cparams(n_axes):
    return pltpu.CompilerParams(dimension_semantics=("arbitrary",) * n_axes,
                                vmem_limit_bytes=VMEM_LIMIT)


def _const_spec(shape):
    nd = len(shape)
    return pl.BlockSpec(shape, lambda *_: (0,) * nd, pipeline_mode=pl.Buffered(1))


def _sigmoid(x):
    return 1.0 / (1.0 + jnp.exp(-x))


def _silu(x):
    return x * _sigmoid(x)


def _log1p_exp_neg_abs(x):
    return jnp.log(1.0 + jnp.exp(-jnp.abs(x)))


def _dot(a, b):
    return jnp.dot(a, b, preferred_element_type=F32)


def _dot_nt(a, b):
    return lax.dot_general(a, b, (((1,), (1,)), ((), ())), preferred_element_type=F32)


def _dot_tn(a, b):
    return lax.dot_general(a, b, (((0,), (0,)), ((), ())), preferred_element_type=F32)


def _group_sum(x, g_ref, split):
    g = g_ref[...]
    w = g.shape[0]

    def halves(xb):
        return jnp.concatenate([_dot(xb[:, i * w:(i + 1) * w], g) for i in range(x.shape[1] // w)], axis=1)

    hi = x.astype(BF16)
    if not split:
        return halves(hi)
    return halves(hi) + halves((x - hi.astype(F32)).astype(BF16))


def _norm_mod(x, nw, shift, scale):
    ms = jnp.mean(x * x, axis=-1, keepdims=True)
    return (x * lax.rsqrt(ms + EPS) * nw) * (1.0 + scale) + shift


def _cumsum_rows(x, n):
    row = lax.broadcasted_iota(jnp.int32, (n, 1), 0)
    s = 1
    while s < n:
        x = x + jnp.where(row >= s, pltpu.roll(x, s, 0), 0.0)
        s *= 2
    return x


def _mod_kernel(c_ref, w_ref, b_ref, o_ref):
    c = c_ref[...]
    cond = _silu(c)
    o_ref[0] = jnp.dot(cond, w_ref[0], preferred_element_type=F32,
                       precision=lax.Precision.HIGHEST) + b_ref[0]


def _modulation(c, mod_w, mod_b):
    L, D, N = mod_w.shape
    B = c.shape[0]
    tn = 1024
    return pl.pallas_call(
        _mod_kernel,
        grid=(L, N // tn),
        in_specs=[pl.BlockSpec((B, D), lambda l, j: (0, 0)),
                  pl.BlockSpec((1, D, tn), lambda l, j: (l, 0, j)),
                  pl.BlockSpec((1, 1, tn), lambda l, j: (l, 0, j))],
        out_specs=pl.BlockSpec((1, B, tn), lambda l, j: (l, 0, j)),
        out_shape=jax.ShapeDtypeStruct((L, B, N), F32),
        compiler_params=_cparams(2),
        name="modulation",
    )(c, mod_w, mod_b.reshape(L, 1, N))


def _rope_kernel(pos_ref, inv_ref, sign_ref, cos_ref, sin_ref):
    ang = pos_ref[0] * inv_ref[...]
    cos_ref[0] = jnp.cos(ang)
    sin_ref[0] = jnp.sin(ang) * sign_ref[...]


def _rope_tables(positions):
    B, S = positions.shape
    ts = 512 if S % 512 == 0 else S
    half = DH_A // 2
    inv = ROPE_THETA ** (-jnp.arange(0, DH_A, 2, dtype=F32) / DH_A)
    inv128 = jnp.tile(inv, LANES // half)[None, :]
    sign128 = jnp.tile(jnp.concatenate([-jnp.ones(half, F32), jnp.ones(half, F32)]), LANES // DH_A)[None, :]
    pos = positions.astype(F32)[..., None]
    return pl.pallas_call(
        _rope_kernel,
        grid=(B, S // ts),
        in_specs=[pl.BlockSpec((1, ts, 1), lambda b, s: (b, s, 0)),
                  pl.BlockSpec((1, LANES), lambda b, s: (0, 0)),
                  pl.BlockSpec((1, LANES), lambda b, s: (0, 0))],
        out_specs=[pl.BlockSpec((1, ts, LANES), lambda b, s: (b, s, 0))] * 2,
        out_shape=[jax.ShapeDtypeStruct((B, S, LANES), F32)] * 2,
        compiler_params=_cparams(2),
        name="rope_tables",
    )(pos, inv128, sign128)


def _ffn_kernel(chunks, shift_i, h_ref, mod_ref, nw_ref, wg_ref, wu_ref, wd_ref, o_ref, act_ref):
    x = h_ref[0]
    m = mod_ref[0]
    u = _norm_mod(x, nw_ref[...], m[shift_i:shift_i + 1], m[shift_i + 1:shift_i + 2]).astype(BF16)
    for lo, hi in chunks:
        g = _dot(u, wg_ref[:, lo:hi])
        up = _dot(u, wu_ref[:, lo:hi])
        act_ref[:, lo:hi] = (_silu(g) * up).astype(BF16)
    y = _dot(act_ref[...], wd_ref[...])
    o_ref[0] = x + (0.5 * m[shift_i + 2:shift_i + 3]) * y


def _ffn(h, mod, nw, wg, wu, wd, shift_i):
    B, S, D = h.shape
    F = wg.shape[1]
    tm = min(FFN_TM, S)
    chunks, lo = [], 0
    while lo < F:
        hi = min(lo + 512, F)
        chunks.append((lo, hi))
        lo = hi
    return pl.pallas_call(
        functools.partial(_ffn_kernel, tuple(chunks), shift_i),
        grid=(B, S // tm),
        in_specs=[pl.BlockSpec((1, tm, D), lambda b, s: (b, s, 0)),
                  pl.BlockSpec((1, N_MOD, D), lambda b, s: (b, 0, 0)),
                  _const_spec((1, D)), _const_spec((D, F)), _const_spec((D, F)), _const_spec((F, D))],
        out_specs=pl.BlockSpec((1, tm, D), lambda b, s: (b, s, 0)),
        out_shape=jax.ShapeDtypeStruct((B, S, D), F32),
        scratch_shapes=[pltpu.VMEM((tm, F), BF16)],
        compiler_params=_cparams(2),
        name="ffn",
    )(h, mod, nw, wg, wu, wd)


_O_QA, _O_KA, _O_VA = 0, W_A, 2 * W_A
_O_B = P_A
_O_C = P_A + P_B_PAD
_O_G = _O_C + P_C


_M_GATES = 0
_M_A = 3 * 1024
_M_B = _M_A + P_A
_M_C = _M_B + 7 * W_B
_M_TOTAL = _M_C + P_C


def _mixin_kernel(h_ref, mod_ref, nw_ref, w_ref, cos_ref, sin_ref, qkw_ref, g64_ref, mu_ref,
                  w0_ref, w2_ref, a0_ref, a2_ref, g2_ref, kk_ref, ka_ref, rk_ref, llb_ref, l1lb_ref,
                  mix_o, logs_o, prev_ref):
    tm = h_ref.shape[1]
    D = h_ref.shape[2]
    x = h_ref[0]
    m = mod_ref[0]
    u = _norm_mod(x, nw_ref[...], m[3:4], m[4:5]).astype(BF16)

    def put(off, val):
        mix_o[0, :, off:off + val.shape[1]] = val.astype(BF16)

    def qk_prep(p, wrow, scale):
        cos4 = jnp.concatenate([cos_ref[0]] * (W_A // LANES), axis=1)
        sin4 = jnp.concatenate([sin_ref[0]] * (W_A // LANES), axis=1)
        lane = lax.broadcasted_iota(jnp.int32, (1, W_A), 1)
        first_half = (lane % DH_A) < (DH_A // 2)
        ss = _group_sum(p * p, g64_ref, False)
        n = p * lax.rsqrt(ss * (1.0 / DH_A) + EPS) * wrow
        rot = jnp.where(first_half, pltpu.roll(n, W_A - DH_A // 2, 1), pltpu.roll(n, DH_A // 2, 1))
        return (n * cos4 + rot * sin4) * scale

    def use_qa(p):
        put(_M_A, qk_prep(p, qkw_ref[0:1], DH_A ** -0.5 * math.log2(math.e)))

    def use_ka(p):
        put(_M_A + W_A, qk_prep(p, qkw_ref[1:2], 1.0))

    def use_va(p):
        put(_M_A + 2 * W_A, p)

    rw = {}

    @pl.when(pl.program_id(1) == 0)
    def _():
        prev_ref[...] = jnp.zeros_like(prev_ref)

    def shift_lerp(pb, lo):
        cols = slice(lo, lo + pb.shape[1])
        row = lax.broadcasted_iota(jnp.int32, (tm, 1), 0)
        shifted = jnp.where(row == 0, prev_ref[:, cols], pltpu.roll(pb, 1, 0))
        prev_ref[:, cols] = pb[tm - 1:tm]
        return pb + (shifted - pb) * mu_ref[:, cols]

    def use_rwkv_r(pb):
        rw["r"] = shift_lerp(pb, 0)
        put(_M_B, rw["r"])

    def use_rwkv_k(pb):
        rw["k"] = shift_lerp(pb, W_B)

    def use_rwkv_v(pb):
        rw["v"] = shift_lerp(pb, 2 * W_B)
        put(_M_B + 2 * W_B, rw["v"])

    def use_rwkv_loras(pb):
        x = shift_lerp(pb, 3 * W_B)
        xwa = x[:, 0:R_W + R_A]
        rw["xg"] = x[:, R_W + R_A:]
        z = w0_ref[...] + _dot(jnp.tanh(xwa).astype(BF16), w2_ref[...])
        logs_o[0, :, 0:W_B] = (-math.exp(-0.5)) * _sigmoid(z)
        rw["a"] = _sigmoid(a0_ref[...] + _dot(xwa.astype(BF16), a2_ref[...]))
        put(_M_B + 5 * W_B, _dot(_sigmoid(rw["xg"]).astype(BF16), g2_ref[...]))

    def use_rwkv_keys():
        k, a = rw["k"], rw["a"]
        kk = k * kk_ref[...]
        kkn = kk / jnp.maximum(jnp.sqrt(_group_sum(kk * kk, g64_ref, False)), 1e-12)
        rw["k2"] = k * (1.0 + (a - 1.0) * ka_ref[...])
        put(_M_B + W_B, rw["k2"])
        put(_M_B + 3 * W_B, kkn)
        put(_M_B + 4 * W_B, kkn * a)

    def use_rwkv_bonus():
        put(_M_B + 6 * W_B, _group_sum(rw["r"] * rw["k2"] * rk_ref[...], g64_ref, False) * rw["v"])

    def use_qc(p):
        put(_M_C, _silu(p))

    def use_fz(fz):
        log_sig = jnp.minimum(fz, 0.0) - _log1p_exp_neg_abs(fz)
        t1 = llb_ref[...]
        t2 = l1lb_ref[...] + log_sig
        logf = jnp.maximum(t1, t2) + _log1p_exp_neg_abs(t1 - t2)
        logs_o[0, :, W_B:W_B + W_C] = logf
        put(_M_C + W_C, 1.0 - jnp.exp(logf))

    def use_ic(p):
        put(_M_C + 2 * W_C, p)

    def use_gc(p):
        put(_M_C + 3 * W_C, _silu(p))

    def use_gates(j):
        return lambda p: put(_M_GATES + j * D, p)

    seg = {"qa": (_O_QA, W_A), "ka": (_O_KA, W_A), "va": (_O_VA, W_A),
           "br": (_O_B, W_B), "bk": (_O_B + W_B, W_B), "bv": (_O_B + 2 * W_B, W_B),
           "bx": (_O_B + 3 * W_B, P_B_PAD - 3 * W_B),
           "qc": (_O_C, W_C), "fz": (_O_C + W_C, W_C), "ic": (_O_C + 2 * W_C, W_C), "gc": (_O_C + 3 * W_C, W_C),
           "g0": (_O_G, D), "g1": (_O_G + D, D), "g2": (_O_G + 2 * D, D)}
    p = {}

    def proj(name):
        lo, width = seg[name]
        p[name] = _dot(u, w_ref[:, lo:lo + width])

    proj("qa")
    proj("ka")
    use_qa(p.pop("qa"))
    proj("bx")
    use_ka(p.pop("ka"))
    proj("bk")
    use_rwkv_loras(p.pop("bx"))
    proj("br")
    use_rwkv_k(p.pop("bk"))
    use_rwkv_keys()
    proj("bv")
    use_rwkv_r(p.pop("br"))
    proj("fz")
    use_rwkv_v(p.pop("bv"))
    use_rwkv_bonus()
    proj("qc")
    use_fz(p.pop("fz"))
    proj("gc")
    use_qc(p.pop("qc"))
    proj("g0")
    use_gc(p.pop("gc"))
    proj("g1")
    use_gates(0)(p.pop("g0"))
    proj("g2")
    use_gates(1)(p.pop("g1"))
    proj("va")
    use_gates(2)(p.pop("g2"))
    proj("ic")
    use_va(p.pop("va"))
    use_ic(p.pop("ic"))


def _mixer_in(h, mod, nw, w_pad, cos, sin, qkw, g64, mu, w0, w2p, a0, a2p, g2p, k_k, k_a, r_k, llb, l1lb):
    B, S, D = h.shape
    tm = min(MIX_TM, S)
    PT = w_pad.shape[1]
    tok = lambda width: pl.BlockSpec((1, tm, width), lambda b, s: (b, s, 0))
    return pl.pallas_call(
        _mixin_kernel,
        grid=(B, S // tm),
        in_specs=[tok(D), pl.BlockSpec((1, N_MOD, D), lambda b, s: (b, 0, 0)), _const_spec((1, D)),
                  _const_spec((D, PT)), tok(LANES), tok(LANES), _const_spec((2, W_A)), _const_spec(g64.shape),
                  _const_spec((1, P_B_PAD)), _const_spec((1, W_B)), _const_spec((R_W + R_A, W_B)),
                  _const_spec((1, W_B)), _const_spec((R_W + R_A, W_B)), _const_spec((R_G_PAD, W_B)),
                  _const_spec((1, W_B)), _const_spec((1, W_B)), _const_spec((1, W_B)),
                  _const_spec((1, W_C)), _const_spec((1, W_C))],
        out_specs=[tok(_M_TOTAL), tok(W_B + W_C)],
        out_shape=[jax.ShapeDtypeStruct((B, S, _M_TOTAL), BF16), jax.ShapeDtypeStruct((B, S, W_B + W_C), F32)],
        scratch_shapes=[pltpu.VMEM((1, P_B_PAD), F32)],
        compiler_params=_cparams(2),
        name="mixer_in",
    )(h, mod, nw, w_pad, cos, sin, qkw, g64, mu, w0, w2p, a0, a2p, g2p, k_k, k_a, r_k, llb, l1lb)


def _attn_kernel(lam_init, t, q_ref, k_ref, v_ref, lq_ref, sw_ref, o_ref, qs_ref, m_ref, l_ref, acc_ref):
    n = q_ref.shape[1] // t
    lane = lax.broadcasted_iota(jnp.int32, (1, LANES), 1)
    for i in range(n):
        q = q_ref[0, i * t:(i + 1) * t, :]
        zero = jnp.zeros_like(q)
        qs_ref[i, 0:t] = jnp.where(lane < DH_A, q, zero)
        qs_ref[i, t:2 * t] = jnp.where(lane < DH_A, zero, q)
    m_ref[...] = jnp.full_like(m_ref, -jnp.inf)
    l_ref[...] = jnp.zeros_like(l_ref)
    acc_ref[...] = jnp.zeros_like(acc_ref)
    lq = lq_ref[...]
    lam = (jnp.exp(jnp.sum(lq[0:1] * lq[1:2], axis=-1, keepdims=True))
           - jnp.exp(jnp.sum(lq[2:3] * lq[3:4], axis=-1, keepdims=True)) + lam_init)

    def scores(i, j):
        return _dot_nt(qs_ref[i], k_ref[0, j * t:(j + 1) * t, :])

    def update(i, j, s):
        if i == j:
            r_i = lax.broadcasted_iota(jnp.int32, (2 * t, t), 0) % t
            c_i = lax.broadcasted_iota(jnp.int32, (2 * t, t), 1)
            s = jnp.where(c_i <= r_i, s, -jnp.inf)
        m_old = m_ref[i]
        m_new = jnp.maximum(m_old, jnp.max(s, axis=-1, keepdims=True))
        alpha = jnp.exp2(m_old - m_new)
        p = jnp.exp2(s - jnp.concatenate([m_new] * (t // LANES), axis=1))
        l_ref[i] = alpha * l_ref[i] + jnp.sum(p, axis=-1, keepdims=True)
        acc_ref[i] = alpha * acc_ref[i] + _dot(p.astype(BF16), v_ref[0, j * t:(j + 1) * t, :])
        m_ref[i] = m_new
        if i == j:
            o = acc_ref[i] / l_ref[i]
            o = o[0:t] - lam * o[t:2 * t]
            ms = jnp.mean(o * o, axis=-1, keepdims=True)
            o_ref[0, i * t:(i + 1) * t, :] = (o * lax.rsqrt(ms + EPS) * sw_ref[...] * (1.0 - lam_init)).astype(BF16)

    chains = [[], []]
    for a in range((n + 1) // 2):
        tiles = [a] if n - 1 - a == a else [a, n - 1 - a]
        shorter = chains[0] if len(chains[0]) <= len(chains[1]) else chains[1]
        shorter += [(i, j) for i in tiles for j in range(i + 1)]
    chain_a, chain_b = chains
    order = []
    for x in range(max(len(chain_a), len(chain_b))):
        order += chain_a[x:x + 1] + chain_b[x:x + 1]
    s_next = scores(*order[0])
    for x, (i, j) in enumerate(order):
        s_cur = s_next
        if x + 1 < len(order):
            s_next = scores(*order[x + 1])
        update(i, j, s_cur)


def _diff_attention(mix, lambda_qk, subln_w, layer_idx):
    B, S, _ = mix.shape
    t = min(ATT_T, S)
    n = S // t
    lam_init = 0.8 - 0.6 * math.exp(-0.3 * layer_idx)
    blk = lambda off: off // LANES
    spec = lambda off: pl.BlockSpec((1, S, LANES), lambda b, h: (b, 0, blk(off) + h))
    return pl.pallas_call(
        functools.partial(_attn_kernel, lam_init, t),
        grid=(B, H_A),
        in_specs=[spec(_M_A), spec(_M_A + W_A), spec(_M_A + 2 * W_A),
                  pl.BlockSpec((4, DH_A), lambda b, h: (0, 0)),
                  pl.BlockSpec((1, 2 * DH_A), lambda b, h: (0, 0))],
        out_specs=pl.BlockSpec((1, S, LANES), lambda b, h: (b, 0, h)),
        out_shape=jax.ShapeDtypeStruct((B, S, W_A), BF16),
        scratch_shapes=[pltpu.VMEM((n, 2 * t, LANES), BF16), pltpu.VMEM((n, 2 * t, LANES), F32),
                        pltpu.VMEM((n, 2 * t, LANES), F32), pltpu.VMEM((n, 2 * t, LANES), F32)],
        compiler_params=_cparams(2),
        name="diff_attention",
    )(mix, mix, mix, lambda_qk, subln_w)


def _rwkv_kernel(r_ref, k_ref, v_ref, kk_ref, beta_ref, lw_ref, g_ref, bonus_ref, lnw_ref, lnb_ref, g64_ref,
                 o_ref, state_ref):
    C = RWKV_C
    C2 = 2 * C

    @pl.when(pl.program_id(1) == 0)
    def _():
        state_ref[...] = jnp.zeros_like(state_ref)

    lane = lax.broadcasted_iota(jnp.int32, (1, LANES), 1)
    head0 = lane < N_B
    ri = lax.broadcasted_iota(jnp.int32, (C, 1), 0)
    ci = lax.broadcasted_iota(jnp.int32, (1, C2), 1) % C
    strict = ci < ri
    incl = ci <= ri
    same_state = (lax.broadcasted_iota(jnp.int32, (LANES, 1), 0) < N_B) == head0

    def stack(x):
        z = jnp.zeros_like(x)
        return jnp.concatenate([jnp.where(head0, x, z), jnp.where(head0, z, x)], axis=0)

    n_pairs = H_B // 2
    n_seq = r_ref.shape[0]

    def st_prep(g):
        bi = g["bi"]
        lw = lw_ref[bi]
        cum = _cumsum_rows(lw, C)
        c_last = cum[C - 1:C]
        p_inv = jnp.exp(-cum)
        p_end = jnp.exp(c_last - cum)
        p_last = jnp.exp(c_last)
        k = k_ref[bi].astype(F32)
        beta = beta_ref[bi].astype(F32)
        v_all = v_ref[bi]
        rt = (r_ref[bi].astype(F32) * jnp.exp(cum)).astype(BF16)
        kt = (kk_ref[bi].astype(F32) * jnp.exp(cum - lw)).astype(BF16)
        kh = (k * p_inv).astype(BF16)
        bh = (beta * p_inv).astype(BF16)
        ke = (k * p_end).astype(BF16)
        be = (beta * p_end).astype(BF16)
        g["chains"] = []
        for p in range(n_pairs):
            sl = slice(p * LANES, (p + 1) * LANES)
            v = v_all[:, sl]
            g["chains"].append(dict(
                ktrt=jnp.concatenate([kt[:, sl], rt[:, sl]], axis=0),
                rhs4=jnp.concatenate([stack(kh[:, sl]), stack(bh[:, sl])], axis=0),
                v=v, v_s=stack(v),
                kebe=jnp.concatenate([ke[:, sl], -be[:, sl]], axis=0),
                p_last=p_last[:, sl]))

    def st_a(g):
        for d in g["chains"]:
            a_all = _dot_nt(d["ktrt"], d.pop("rhs4"))
            d["a_kk"] = jnp.where(strict, a_all[0:C, 0:C2], 0.0).astype(BF16)
            d["a_o"] = jnp.concatenate([jnp.where(incl, a_all[C:, 0:C2], 0.0),
                                        -jnp.where(incl, a_all[C:, C2:], 0.0)], axis=1).astype(BF16)
            d["xm"] = -jnp.where(strict, a_all[0:C, C2:], 0.0)

    def st_p1(g):
        for d in g["chains"]:
            pw = d["xm"].astype(BF16)
            d["pf"] = _dot(pw, stack(pw))

    def st_neumann(s):
        def stage(g):
            for d in g["chains"]:
                pf = d["pf"]
                pw = pf.astype(BF16)
                xs = stack(d["xm"].astype(BF16))
                if 2 * s >= C:
                    d["xm"] = d["xm"] + pf + _dot(pw, xs)
                else:
                    both = _dot(pw, jnp.concatenate([stack(pw), xs], axis=1))
                    d["xm"] = d["xm"] + pf + both[:, C2:]
                    d["pf"] = both[:, 0:C2]
        return stage

    def st_state_in(g):
        for p, d in enumerate(g["chains"]):
            d["av"] = _dot(d.pop("a_kk"), d["v_s"])
            d["xm"] = d["xm"].astype(BF16)
            d["st"] = state_ref[g["bi"] * n_pairs + p]
            d["w1r1"] = _dot_nt(d["ktrt"], d["st"].astype(BF16))

    def st_u(g):
        for d in g["chains"]:
            rhs = d["w1r1"][0:C] + d["av"]
            d["u"] = (rhs + _dot(d["xm"], stack(rhs.astype(BF16)))).astype(BF16)

    def st_out(g):
        outs = []
        for p, d in enumerate(g["chains"]):
            u = d["u"]
            outs.append(d["w1r1"][C:C2] + _dot(d["a_o"], jnp.concatenate([d["v_s"], stack(u)], axis=0)))
            upd = _dot_tn(jnp.concatenate([d["v"], u], axis=0), d["kebe"])
            state_ref[g["bi"] * n_pairs + p] = d["st"] * d["p_last"] + jnp.where(same_state, upd, 0.0)
        g["o"] = jnp.concatenate(outs, axis=1)

    def st_center(g):
        g["d"] = g["o"] - _group_sum(g["o"], g64_ref, True) * (1.0 / N_B)

    def st_norm(g):
        bi, d = g["bi"], g["d"]
        var = _group_sum(d * d, g64_ref, False) * (1.0 / N_B)
        y = d * lax.rsqrt(var + RWKV_LN_EPS) * lnw_ref[...] + lnb_ref[...] + bonus_ref[bi].astype(F32)
        o_ref[bi] = (y * g_ref[bi].astype(F32)).astype(BF16)

    stages = [st_prep, st_a, st_p1]
    s = 2
    while s < C:
        stages.append(st_neumann(s))
        s *= 2
    stages += [st_state_in, st_u, st_out, st_center, st_norm]
    groups = [dict(bi=bi) for bi in range(n_seq)]
    for tau in range(len(stages) + len(groups) - 1):
        for gi, g in enumerate(groups):
            if 0 <= tau - gi < len(stages):
                stages[tau - gi](g)


def _rwkv_scan(mix, logs, ln_w, ln_b, g64):
    B, S, _ = mix.shape
    W = W_B
    t = RWKV_C
    nb = RWKV_NB if B % RWKV_NB == 0 else 1
    tok = lambda j: pl.BlockSpec((nb, t, W), lambda b, c: (b, c, j))
    m0 = _M_B // W
    return pl.pallas_call(
        _rwkv_kernel,
        grid=(B // nb, S // t),
        in_specs=[tok(m0 + j) for j in range(5)] + [tok(0), tok(m0 + 5), tok(m0 + 6)]
                 + [_const_spec((1, W)), _const_spec((1, W)), _const_spec(g64.shape)],
        out_specs=tok(0),
        out_shape=jax.ShapeDtypeStruct((B, S, W), BF16),
        scratch_shapes=[pltpu.VMEM((nb * H_B // 2, LANES, LANES), F32)],
        compiler_params=_cparams(2),
        name="rwkv7_scan",
    )(mix, mix, mix, mix, mix, logs, mix, mix, ln_w, ln_b, g64)


def _hgrn_kernel(q_ref, k_ref, v_ref, lf_ref, gate_ref, nw_ref, o_ref, state_ref):
    C = HGRN_C
    c = HGRN_SUB
    n_chunks = q_ref.shape[1] // C

    @pl.when(pl.program_id(1) == 0)
    def _():
        state_ref[...] = jnp.zeros_like(state_ref)

    lane_c = lax.broadcasted_iota(jnp.int32, (1, c), 1)
    row_c = lax.broadcasted_iota(jnp.int32, (c, 1), 0)

    units = []
    for n in range(n_chunks):
        rows = slice(n * C, (n + 1) * C)
        b = _cumsum_rows(lf_ref[0, rows, :], C) * math.log2(math.e)
        b_end = b[C - 1:C]
        q = q_ref[0, rows, :].astype(F32)
        k = k_ref[0, rows, :].astype(F32)
        v_all = v_ref[0, rows, :]
        q_in = (q * jnp.exp2(b)).astype(BF16)
        k_end = (k * jnp.exp2(b_end - b)).astype(BF16)
        p_end = jnp.exp2(b_end)
        for h in range(H_C):
            sl = slice(h * LANES, (h + 1) * LANES)
            units.append(dict(b=b[:, sl], q=q[:, sl], k=k[:, sl], v=v_all[:, sl], q_in=q_in[:, sl],
                              k_end=k_end[:, sl], p_end=p_end[:, sl]))

    for d in units:
        d["cross"] = []
        for i in range(1, C // c):
            r0 = i * c
            b_ref_ = d["b"][r0 - 1:r0]
            qs = (d["q"][r0:r0 + c] * jnp.exp2(d["b"][r0:r0 + c] - b_ref_)).astype(BF16)
            ks = (d["k"][0:r0] * jnp.exp2(b_ref_ - d["b"][0:r0])).astype(BF16)
            d["cross"].append(_dot_nt(qs, ks).astype(BF16))
    for d in units:
        d["diag"] = []
        for i in range(C // c):
            r0 = i * c
            b_i = d["b"][r0:r0 + c]
            q_i = d["q"][r0:r0 + c]
            k_i = d["k"][r0:r0 + c]
            sc = jnp.zeros((c, c), F32)
            for j in range(c):
                e = jnp.exp2(b_i - b_i[j:j + 1])
                col = jnp.sum(q_i * e * k_i[j:j + 1], axis=-1, keepdims=True)
                sc = jnp.where(lane_c == j, col, sc)
            d["diag"].append(jnp.where(lane_c <= row_c, sc, 0.0).astype(BF16))
    for d in units:
        blocks = []
        for i in range(C // c):
            r0 = i * c
            o_i = _dot(d["diag"][i], d["v"][r0:r0 + c])
            if i > 0:
                o_i = o_i + _dot(d["cross"][i - 1], d["v"][0:r0])
            blocks.append(o_i)
        d["intra"] = jnp.concatenate(blocks, axis=0)

    state = [state_ref[h] for h in range(H_C)]
    o_chunks = []
    for n in range(n_chunks):
        us = units[n * H_C:(n + 1) * H_C]
        o_heads = [_dot_nt(d["q_in"], st.astype(BF16)) + d["intra"] for d, st in zip(us, state)]
        state = [st * d["p_end"] + _dot_tn(d["v"], d["k_end"]) for d, st in zip(us, state)]
        o_chunks.append(jnp.concatenate(
            [o_h * lax.rsqrt(jnp.mean(o_h * o_h, axis=-1, keepdims=True) + EPS) * nw_ref[...] for o_h in o_heads],
            axis=1))
    for h in range(H_C):
        state_ref[h] = state[h]
    o = jnp.concatenate(o_chunks, axis=0)
    o_ref[0] = (o * gate_ref[0].astype(F32)).astype(BF16)


def _hgrn_scan(mix, logs, norm_w):
    B, S, _ = mix.shape
    W = W_C
    t = min(HGRN_T, S)
    tok = lambda j: pl.BlockSpec((1, t, W), lambda b, c: (b, c, j))
    m0 = _M_C // W
    return pl.pallas_call(
        _hgrn_kernel,
        grid=(B, S // t),
        in_specs=[tok(m0), tok(m0 + 1), tok(m0 + 2), tok(W_B // W), tok(m0 + 3), _const_spec((1, LANES))],
        out_specs=tok(0),
        out_shape=jax.ShapeDtypeStruct((B, S, W), BF16),
        scratch_shapes=[pltpu.VMEM((H_C, LANES, LANES), F32)],
        compiler_params=_cparams(2),
        name="hgrn2_scan",
    )(mix, mix, mix, logs, mix, norm_w)


def _merge_kernel(h_ref, mod_ref, oa_ref, ob_ref, oc_ref, gates_ref, wa_ref, wb_ref, wc_ref, wo_ref, o_ref):
    D = h_ref.shape[2]
    gate = lambda j: _sigmoid(gates_ref[0, :, j * D:(j + 1) * D].astype(F32))
    z = (gate(0) * _dot(oa_ref[0], wa_ref[...])
         + gate(1) * _dot(ob_ref[0], wb_ref[...])
         + gate(2) * _dot(oc_ref[0], wc_ref[...]))
    y = _dot(z.astype(BF16), wo_ref[...])
    o_ref[0] = h_ref[0] + mod_ref[0][5:6] * y


def _merge(h, mod, oa, ob, oc, mix, wa, wb, wc, wo):
    B, S, D = h.shape
    tm = min(MERGE_TM, S)
    tok = lambda width: pl.BlockSpec((1, tm, width), lambda b, s: (b, s, 0))
    return pl.pallas_call(
        _merge_kernel,
        grid=(B, S // tm),
        in_specs=[tok(D), pl.BlockSpec((1, N_MOD, D), lambda b, s: (b, 0, 0)),
                  tok(W_A), tok(W_B), tok(W_C), tok(3 * D),
                  _const_spec((W_A, D)), _const_spec((W_B, D)), _const_spec((W_C, D)), _const_spec((D, D))],
        out_specs=tok(D),
        out_shape=jax.ShapeDtypeStruct((B, S, D), F32),
        compiler_params=_cparams(2),
        name="merge",
    )(h, mod, oa, ob, oc, mix, wa, wb, wc, wo)


def _block_ones(width, group):
    i = jnp.arange(width) // group
    return (i[:, None] == i[None, :]).astype(BF16)


def kernel(x, c, positions, mod_w, mod_b, norm_w, ffn_w_gate, ffn_w_up, ffn_w_down, w_in, qk_norm_w, lambda_qk,
           subln_w, w_out_a, rwkv_mu, rwkv_w0, rwkv_w2, rwkv_a0, rwkv_a2, rwkv_g2, rwkv_k_k, rwkv_k_a, rwkv_r_k,
           rwkv_ln_w, rwkv_ln_b, w_out_b, hgrn_lower_bounds, hgrn_norm_w, w_out_c, w_out):
    B, S, D = x.shape
    L = mod_w.shape[0]
    lb = jnp.cumsum(jax.nn.softmax(hgrn_lower_bounds.astype(F32), axis=0), axis=0)
    lb = lb - lb[0]
    log_lb = jnp.log(lb)
    log1m_lb = jnp.log1p(-lb)

    mod = _modulation(c, mod_w, mod_b).reshape(L, B, N_MOD, D)
    cos, sin = _rope_tables(positions)
    g64 = _block_ones(2 * LANES, N_B)
    pad_cols = R_G_PAD - R_G
    zrow = lambda n: jnp.zeros((n, W_B), F32)

    h = x
    for l in range(L):
        wg = ffn_w_gate[l].astype(BF16)
        wu = ffn_w_up[l].astype(BF16)
        wd = ffn_w_down[l].astype(BF16)
        h = _ffn(h, mod[l], norm_w[l, 0][None], wg[0], wu[0], wd[0], 0)

        w_pad = jnp.concatenate([w_in[l][:, :P_A + P_B], jnp.zeros((D, pad_cols), F32),
                                 w_in[l][:, P_A + P_B:]], axis=1).astype(BF16)
        mu = jnp.concatenate([rwkv_mu[l], jnp.zeros((pad_cols,), F32)])[None]
        w2p = jnp.concatenate([rwkv_w2[l], zrow(R_A)], axis=0).astype(BF16)
        a2p = jnp.concatenate([zrow(R_W), rwkv_a2[l]], axis=0).astype(BF16)
        g2p = jnp.concatenate([rwkv_g2[l], zrow(pad_cols)], axis=0).astype(BF16)
        qkw = jnp.tile(qk_norm_w[l], (1, W_A // DH_A))
        mix, logs = _mixer_in(
            h, mod[l], norm_w[l, 1][None], w_pad, cos, sin, qkw, g64, mu,
            rwkv_w0[l][None], w2p, rwkv_a0[l][None], a2p, g2p, rwkv_k_k[l][None], rwkv_k_a[l][None],
            rwkv_r_k[l].reshape(1, W_B), log_lb[l][None], log1m_lb[l][None])
        oa = _diff_attention(mix, lambda_qk[l], subln_w[l][None], l)
        ob = _rwkv_scan(mix, logs, rwkv_ln_w[l][None], rwkv_ln_b[l][None], g64)
        oc = _hgrn_scan(mix, logs, hgrn_norm_w[l][None])
        h = _merge(h, mod[l], oa, ob, oc, mix, w_out_a[l].astype(BF16), w_out_b[l].astype(BF16),
                   w_out_c[l].astype(BF16), w_out[l].astype(BF16))

        h = _ffn(h, mod[l], norm_w[l, 2][None], wg[1], wu[1], wd[1], 6)
    return h
```

```python
import functools
import math

import jax
import jax.numpy as jnp
from jax import lax
from jax.experimental import pallas as pl
from jax.experimental.pallas import tpu as pltpu

F32 = jnp.float32
BF16 = jnp.bfloat16

H_A, DH_A = 4, 64
W_A = H_A * 2 * DH_A
ROPE_THETA = 10000.0
H_B, N_B = 8, 64
W_B = H_B * N_B
R_W, R_A, R_G = 64, 64, 160
RWKV_LN_EPS = 64e-5
H_C, DK_C = 4, 128
W_C = H_C * DK_C
N_MOD = 9
EPS = 1e-6
P_A = 3 * W_A
P_B = 3 * W_B + R_W + R_A + R_G
P_C = 4 * W_C

LANES = 128
R_G_PAD = 256
P_B_PAD = 3 * W_B + R_W + R_A + R_G_PAD
VMEM_LIMIT = 56 * 1024 * 1024

FFN_TM = 1024
MIX_TM = 256
ATT_T = 512
RWKV_C = 64
RWKV_NB = 4
HGRN_C = 64
HGRN_T = 256
HGRN_SUB = 8
MERGE_TM = 1024


def _cparams(n_axes):
    return pltpu.CompilerParams(dimension_semantics=("arbitrary",) * n_axes,
                                vmem_limit_bytes=VMEM_LIMIT)


def _pick(arr, *lead):
    return arr, lead


def _pick_spec(p):
    arr, lead = p
    rest = arr.shape[len(lead):]
    index = tuple(lead) + (0,) * len(rest)
    return pl.BlockSpec((None,) * len(lead) + tuple(rest), lambda *_: index, pipeline_mode=pl.Buffered(1))


def _mod_spec(mod, l):
    return pl.BlockSpec((None, 1) + mod.shape[2:], lambda b, s: (l, b, 0, 0))


def _sigmoid(x):
    return 1.0 / (1.0 + jnp.exp(-x))


def _silu(x):
    return x * _sigmoid(x)


def _log1p_exp_neg_abs(x):
    return jnp.log(1.0 + jnp.exp(-jnp.abs(x)))


def _dot(a, b):
    return jnp.dot(a, b, preferred_element_type=F32)


def _dot_nt(a, b):
    return lax.dot_general(a, b, (((1,), (1,)), ((), ())), preferred_element_type=F32)


def _dot_tn(a, b):
    return lax.dot_general(a, b, (((0,), (0,)), ((), ())), preferred_element_type=F32)


def _group_sum(x, g_ref, split):
    g = g_ref[...]
    w = g.shape[0]

    def halves(xb):
        return jnp.concatenate([_dot(xb[:, i * w:(i + 1) * w], g) for i in range(x.shape[1] // w)], axis=1)

    hi = x.astype(BF16)
    if not split:
        return halves(hi)
    return halves(hi) + halves((x - hi.astype(F32)).astype(BF16))


def _norm_mod(x, nw, shift, scale):
    ms = jnp.mean(x * x, axis=-1, keepdims=True)
    return (x * lax.rsqrt(ms + EPS) * nw) * (1.0 + scale) + shift


def _cumsum_rows(x, n):
    row = lax.broadcasted_iota(jnp.int32, (n, 1), 0)
    s = 1
    while s < n:
        x = x + jnp.where(row >= s, pltpu.roll(x, s, 0), 0.0)
        s *= 2
    return x


def _mod_kernel(c_ref, w_ref, b_ref, o_ref):
    c = c_ref[...]
    cond = _silu(c)
    o_ref[0] = jnp.dot(cond, w_ref[0], preferred_element_type=F32,
                       precision=lax.Precision.HIGHEST) + b_ref[0]


def _modulation(c, mod_w, mod_b):
    L, D, N = mod_w.shape
    B = c.shape[0]
    tn = 1024
    return pl.pallas_call(
        _mod_kernel,
        grid=(L, N // tn),
        in_specs=[pl.BlockSpec((B, D), lambda l, j: (0, 0)),
                  pl.BlockSpec((1, D, tn), lambda l, j: (l, 0, j)),
                  pl.BlockSpec((1, 1, tn), lambda l, j: (l, 0, j))],
        out_specs=pl.BlockSpec((1, B, tn), lambda l, j: (l, 0, j)),
        out_shape=jax.ShapeDtypeStruct((L, B, N), F32),
        compiler_params=_cparams(2),
        name="modulation",
    )(c, mod_w, mod_b.reshape(L, 1, N))


def _rope_kernel(pos_ref, inv_ref, sign_ref, cos_ref, sin_ref):
    ang = pos_ref[0] * inv_ref[...]
    cos_ref[0] = jnp.cos(ang)
    sin_ref[0] = jnp.sin(ang) * sign_ref[...]


def _rope_tables(positions):
    B, S = positions.shape
    ts = 512 if S % 512 == 0 else S
    half = DH_A // 2
    inv = ROPE_THETA ** (-jnp.arange(0, DH_A, 2, dtype=F32) / DH_A)
    inv128 = jnp.tile(inv, LANES // half)[None, :]
    sign128 = jnp.tile(jnp.concatenate([-jnp.ones(half, F32), jnp.ones(half, F32)]), LANES // DH_A)[None, :]
    pos = positions.astype(F32)[..., None]
    return pl.pallas_call(
        _rope_kernel,
        grid=(B, S // ts),
        in_specs=[pl.BlockSpec((1, ts, 1), lambda b, s: (b, s, 0)),
                  pl.BlockSpec((1, LANES), lambda b, s: (0, 0)),
                  pl.BlockSpec((1, LANES), lambda b, s: (0, 0))],
        out_specs=[pl.BlockSpec((1, ts, LANES), lambda b, s: (b, s, 0))] * 2,
        out_shape=[jax.ShapeDtypeStruct((B, S, LANES), F32)] * 2,
        compiler_params=_cparams(2),
        name="rope_tables",
    )(pos, inv128, sign128)


def _ffn_kernel(chunks, shift_i, h_ref, mod_ref, nw_ref, wg_ref, wu_ref, wd_ref, o_ref, act_ref):
    x = h_ref[0]
    m = mod_ref[0]
    u = _norm_mod(x, nw_ref[...], m[shift_i:shift_i + 1], m[shift_i + 1:shift_i + 2]).astype(BF16)
    for lo, hi in chunks:
        g = _dot(u, wg_ref[:, lo:hi])
        up = _dot(u, wu_ref[:, lo:hi])
        act_ref[:, lo:hi] = (_silu(g) * up).astype(BF16)
    y = _dot(act_ref[...], wd_ref[...])
    o_ref[0] = x + (0.5 * m[shift_i + 2:shift_i + 3]) * y


def _ffn(h, mod, l, nw, wg, wu, wd, shift_i):
    B, S, D = h.shape
    F = wg[0].shape[-1]
    tm = min(FFN_TM, S)
    chunks, lo = [], 0
    while lo < F:
        hi = min(lo + 512, F)
        chunks.append((lo, hi))
        lo = hi
    return pl.pallas_call(
        functools.partial(_ffn_kernel, tuple(chunks), shift_i),
        grid=(B, S // tm),
        in_specs=[pl.BlockSpec((1, tm, D), lambda b, s: (b, s, 0)),
                  _mod_spec(mod, l), _pick_spec(nw), _pick_spec(wg), _pick_spec(wu), _pick_spec(wd)],
        out_specs=pl.BlockSpec((1, tm, D), lambda b, s: (b, s, 0)),
        out_shape=jax.ShapeDtypeStruct((B, S, D), F32),
        scratch_shapes=[pltpu.VMEM((tm, F), BF16)],
        compiler_params=_cparams(2),
        name="ffn",
    )(h, mod, nw[0], wg[0], wu[0], wd[0])


_O_QA, _O_KA, _O_VA = 0, W_A, 2 * W_A
_O_B = P_A
_O_C = P_A + P_B_PAD
_O_G = _O_C + P_C


_M_GATES = 0
_M_A = 3 * 1024
_M_B = _M_A + P_A
_M_C = _M_B + 7 * W_B
_M_TOTAL = _M_C + P_C


def _mixin_kernel(h_ref, mod_ref, nw_ref, w_ref, cos_ref, sin_ref, qkw_ref, g64_ref, mu_ref,
                  w0_ref, w2_ref, a0_ref, a2_ref, g2_ref, kk_ref, ka_ref, rk_ref, llb_ref, l1lb_ref,
                  mix_o, logs_o, prev_ref):
    tm = h_ref.shape[1]
    D = h_ref.shape[2]
    x = h_ref[0]
    m = mod_ref[0]
    u = _norm_mod(x, nw_ref[...], m[3:4], m[4:5]).astype(BF16)

    def put(off, val):
        mix_o[0, :, off:off + val.shape[1]] = val.astype(BF16)

    def qk_prep(p, wrow, scale):
        cos4 = jnp.concatenate([cos_ref[0]] * (W_A // LANES), axis=1)
        sin4 = jnp.concatenate([sin_ref[0]] * (W_A // LANES), axis=1)
        lane = lax.broadcasted_iota(jnp.int32, (1, W_A), 1)
        first_half = (lane % DH_A) < (DH_A // 2)
        ss = _group_sum(p * p, g64_ref, False)
        n = p * lax.rsqrt(ss * (1.0 / DH_A) + EPS) * wrow
        rot = jnp.where(first_half, pltpu.roll(n, W_A - DH_A // 2, 1), pltpu.roll(n, DH_A // 2, 1))
        return (n * cos4 + rot * sin4) * scale

    def use_qa(p):
        put(_M_A, qk_prep(p, qkw_ref[0:1], DH_A ** -0.5 * math.log2(math.e)))

    def use_ka(p):
        put(_M_A + W_A, qk_prep(p, qkw_ref[1:2], 1.0))

    def use_va(p):
        put(_M_A + 2 * W_A, p)

    rw = {}

    @pl.when(pl.program_id(1) == 0)
    def _():
        prev_ref[...] = jnp.zeros_like(prev_ref)

    def shift_lerp(pb, lo):
        cols = slice(lo, lo + pb.shape[1])
        row = lax.broadcasted_iota(jnp.int32, (tm, 1), 0)
        shifted = jnp.where(row == 0, prev_ref[:, cols], pltpu.roll(pb, 1, 0))
        prev_ref[:, cols] = pb[tm - 1:tm]
        return pb + (shifted - pb) * mu_ref[:, cols]

    def use_rwkv_r(pb):
        rw["r"] = shift_lerp(pb, 0)
        put(_M_B, rw["r"])

    def use_rwkv_k(pb):
        rw["k"] = shift_lerp(pb, W_B)

    def use_rwkv_v(pb):
        rw["v"] = shift_lerp(pb, 2 * W_B)
        put(_M_B + 2 * W_B, rw["v"])

    def use_rwkv_loras(pb):
        x = shift_lerp(pb, 3 * W_B)
        xwa = x[:, 0:R_W + R_A]
        rw["xg"] = x[:, R_W + R_A:]
        z = w0_ref[...] + _dot(jnp.tanh(xwa).astype(BF16), w2_ref[...])
        logs_o[0, :, 0:W_B] = (-math.exp(-0.5)) * _sigmoid(z)
        rw["a"] = _sigmoid(a0_ref[...] + _dot(xwa.astype(BF16), a2_ref[...]))
        put(_M_B + 5 * W_B, _dot(_sigmoid(rw["xg"]).astype(BF16), g2_ref[...]))

    def use_rwkv_keys():
        k, a = rw["k"], rw["a"]
        kk = k * kk_ref[...]
        kkn = kk / jnp.maximum(jnp.sqrt(_group_sum(kk * kk, g64_ref, False)), 1e-12)
        rw["k2"] = k * (1.0 + (a - 1.0) * ka_ref[...])
        put(_M_B + W_B, rw["k2"])
        put(_M_B + 3 * W_B, kkn)
        put(_M_B + 4 * W_B, kkn * a)

    def use_rwkv_bonus():
        put(_M_B + 6 * W_B, _group_sum(rw["r"] * rw["k2"] * rk_ref[...], g64_ref, False) * rw["v"])

    def use_qc(p):
        put(_M_C, _silu(p))

    def use_fz(fz):
        log_sig = jnp.minimum(fz, 0.0) - _log1p_exp_neg_abs(fz)
        t1 = llb_ref[...]
        t2 = l1lb_ref[...] + log_sig
        logf = jnp.maximum(t1, t2) + _log1p_exp_neg_abs(t1 - t2)
        logs_o[0, :, W_B:W_B + W_C] = logf
        put(_M_C + W_C, 1.0 - jnp.exp(logf))

    def use_ic(p):
        put(_M_C + 2 * W_C, p)

    def use_gc(p):
        put(_M_C + 3 * W_C, _silu(p))

    def use_gates(j):
        return lambda p: put(_M_GATES + j * D, p)

    seg = {"qa": (_O_QA, W_A), "ka": (_O_KA, W_A), "va": (_O_VA, W_A),
           "br": (_O_B, W_B), "bk": (_O_B + W_B, W_B), "bv": (_O_B + 2 * W_B, W_B),
           "bx": (_O_B + 3 * W_B, P_B_PAD - 3 * W_B),
           "qc": (_O_C, W_C), "fz": (_O_C + W_C, W_C), "ic": (_O_C + 2 * W_C, W_C), "gc": (_O_C + 3 * W_C, W_C),
           "g0": (_O_G, D), "g1": (_O_G + D, D), "g2": (_O_G + 2 * D, D)}
    p = {}

    def proj(name):
        lo, width = seg[name]
        p[name] = _dot(u, w_ref[:, lo:lo + width])

    proj("qa")
    proj("ka")
    use_qa(p.pop("qa"))
    proj("bx")
    use_ka(p.pop("ka"))
    proj("bk")
    use_rwkv_loras(p.pop("bx"))
    proj("br")
    use_rwkv_k(p.pop("bk"))
    use_rwkv_keys()
    proj("bv")
    use_rwkv_r(p.pop("br"))
    proj("fz")
    use_rwkv_v(p.pop("bv"))
    use_rwkv_bonus()
    proj("qc")
    use_fz(p.pop("fz"))
    proj("gc")
    use_qc(p.pop("qc"))
    proj("g0")
    use_gc(p.pop("gc"))
    proj("g1")
    use_gates(0)(p.pop("g0"))
    proj("g2")
    use_gates(1)(p.pop("g1"))
    proj("va")
    use_gates(2)(p.pop("g2"))
    proj("ic")
    use_va(p.pop("va"))
    use_ic(p.pop("ic"))


def _mixer_in(h, mod, l, cos, sin, params):
    B, S, D = h.shape
    tm = min(MIX_TM, S)
    nw, w_pad = params[0], params[1]
    tok = lambda width: pl.BlockSpec((1, tm, width), lambda b, s: (b, s, 0))
    return pl.pallas_call(
        _mixin_kernel,
        grid=(B, S // tm),
        in_specs=[tok(D), _mod_spec(mod, l), _pick_spec(nw), _pick_spec(w_pad), tok(LANES), tok(LANES)]
                 + [_pick_spec(p) for p in params[2:]],
        out_specs=[tok(_M_TOTAL), tok(W_B + W_C)],
        out_shape=[jax.ShapeDtypeStruct((B, S, _M_TOTAL), BF16), jax.ShapeDtypeStruct((B, S, W_B + W_C), F32)],
        scratch_shapes=[pltpu.VMEM((1, P_B_PAD), F32)],
        compiler_params=_cparams(2),
        name="mixer_in",
    )(h, mod, nw[0], w_pad[0], cos, sin, *[p[0] for p in params[2:]])


def _attn_kernel(lam_init, t, q_ref, k_ref, v_ref, lq_ref, sw_ref, o_ref, qs_ref, m_ref, l_ref, acc_ref):
    n = q_ref.shape[1] // t
    lane = lax.broadcasted_iota(jnp.int32, (1, LANES), 1)
    for i in range(n):
        q = q_ref[0, i * t:(i + 1) * t, :]
        zero = jnp.zeros_like(q)
        qs_ref[i, 0:t] = jnp.where(lane < DH_A, q, zero)
        qs_ref[i, t:2 * t] = jnp.where(lane < DH_A, zero, q)
    m_ref[...] = jnp.full_like(m_ref, -jnp.inf)
    l_ref[...] = jnp.zeros_like(l_ref)
    acc_ref[...] = jnp.zeros_like(acc_ref)
    lq = lq_ref[...]
    lam = (jnp.exp(jnp.sum(lq[0:1] * lq[1:2], axis=-1, keepdims=True))
           - jnp.exp(jnp.sum(lq[2:3] * lq[3:4], axis=-1, keepdims=True)) + lam_init)

    def scores(i, j):
        return _dot_nt(qs_ref[i], k_ref[0, j * t:(j + 1) * t, :])

    def update(i, j, s):
        if i == j:
            r_i = lax.broadcasted_iota(jnp.int32, (2 * t, t), 0) % t
            c_i = lax.broadcasted_iota(jnp.int32, (2 * t, t), 1)
            s = jnp.where(c_i <= r_i, s, -jnp.inf)
        m_old = m_ref[i]
        m_new = jnp.maximum(m_old, jnp.max(s, axis=-1, keepdims=True))
        alpha = jnp.exp2(m_old - m_new)
        p = jnp.exp2(s - jnp.concatenate([m_new] * (t // LANES), axis=1))
        l_ref[i] = alpha * l_ref[i] + jnp.sum(p, axis=-1, keepdims=True)
        acc_ref[i] = alpha * acc_ref[i] + _dot(p.astype(BF16), v_ref[0, j * t:(j + 1) * t, :])
        m_ref[i] = m_new
        if i == j:
            o = acc_ref[i] / l_ref[i]
            o = o[0:t] - lam * o[t:2 * t]
            ms = jnp.mean(o * o, axis=-1, keepdims=True)
            o_ref[0, i * t:(i + 1) * t, :] = (o * lax.rsqrt(ms + EPS) * sw_ref[...] * (1.0 - lam_init)).astype(BF16)

    chains = [[], []]
    for a in range((n + 1) // 2):
        tiles = [a] if n - 1 - a == a else [a, n - 1 - a]
        shorter = chains[0] if len(chains[0]) <= len(chains[1]) else chains[1]
        shorter += [(i, j) for i in tiles for j in range(i + 1)]
    chain_a, chain_b = chains
    order = []
    for x in range(max(len(chain_a), len(chain_b))):
        order += chain_a[x:x + 1] + chain_b[x:x + 1]
    s_next = scores(*order[0])
    for x, (i, j) in enumerate(order):
        s_cur = s_next
        if x + 1 < len(order):
            s_next = scores(*order[x + 1])
        update(i, j, s_cur)


def _diff_attention(mix, lambda_qk, subln_w, layer_idx):
    B, S, _ = mix.shape
    t = min(ATT_T, S)
    n = S // t
    lam_init = 0.8 - 0.6 * math.exp(-0.3 * layer_idx)
    blk = lambda off: off // LANES
    spec = lambda off: pl.BlockSpec((1, S, LANES), lambda b, h: (b, 0, blk(off) + h))
    return pl.pallas_call(
        functools.partial(_attn_kernel, lam_init, t),
        grid=(B, H_A),
        in_specs=[spec(_M_A), spec(_M_A + W_A), spec(_M_A + 2 * W_A),
                  _pick_spec(lambda_qk), _pick_spec(subln_w)],
        out_specs=pl.BlockSpec((1, S, LANES), lambda b, h: (b, 0, h)),
        out_shape=jax.ShapeDtypeStruct((B, S, W_A), BF16),
        scratch_shapes=[pltpu.VMEM((n, 2 * t, LANES), BF16), pltpu.VMEM((n, 2 * t, LANES), F32),
                        pltpu.VMEM((n, 2 * t, LANES), F32), pltpu.VMEM((n, 2 * t, LANES), F32)],
        compiler_params=_cparams(2),
        name="diff_attention",
    )(mix, mix, mix, lambda_qk[0], subln_w[0])


def _rwkv_kernel(r_ref, k_ref, v_ref, kk_ref, beta_ref, lw_ref, g_ref, bonus_ref, lnw_ref, lnb_ref, g64_ref,
                 o_ref, state_ref):
    C = RWKV_C
    C2 = 2 * C

    @pl.when(pl.program_id(1) == 0)
    def _():
        state_ref[...] = jnp.zeros_like(state_ref)

    lane = lax.broadcasted_iota(jnp.int32, (1, LANES), 1)
    head0 = lane < N_B
    ri = lax.broadcasted_iota(jnp.int32, (C, 1), 0)
    ci = lax.broadcasted_iota(jnp.int32, (1, C2), 1) % C
    strict = ci < ri
    incl = ci <= ri
    same_state = (lax.broadcasted_iota(jnp.int32, (LANES, 1), 0) < N_B) == head0

    def stack(x):
        z = jnp.zeros_like(x)
        return jnp.concatenate([jnp.where(head0, x, z), jnp.where(head0, z, x)], axis=0)

    n_pairs = H_B // 2
    n_seq = r_ref.shape[0]

    def st_prep(g):
        bi = g["bi"]
        lw = lw_ref[bi]
        cum = _cumsum_rows(lw, C)
        c_last = cum[C - 1:C]
        p_inv = jnp.exp(-cum)
        p_end = jnp.exp(c_last - cum)
        p_last = jnp.exp(c_last)
        k = k_ref[bi].astype(F32)
        beta = beta_ref[bi].astype(F32)
        v_all = v_ref[bi]
        rt = (r_ref[bi].astype(F32) * jnp.exp(cum)).astype(BF16)
        kt = (kk_ref[bi].astype(F32) * jnp.exp(cum - lw)).astype(BF16)
        kh = (k * p_inv).astype(BF16)
        bh = (beta * p_inv).astype(BF16)
        ke = (k * p_end).astype(BF16)
        be = (beta * p_end).astype(BF16)
        g["chains"] = []
        for p in range(n_pairs):
            sl = slice(p * LANES, (p + 1) * LANES)
            v = v_all[:, sl]
            g["chains"].append(dict(
                ktrt=jnp.concatenate([kt[:, sl], rt[:, sl]], axis=0),
                rhs4=jnp.concatenate([stack(kh[:, sl]), stack(bh[:, sl])], axis=0),
                v=v, v_s=stack(v),
                kebe=jnp.concatenate([ke[:, sl], -be[:, sl]], axis=0),
                p_last=p_last[:, sl]))

    def st_a(g):
        for d in g["chains"]:
            a_all = _dot_nt(d["ktrt"], d.pop("rhs4"))
            d["a_kk"] = jnp.where(strict, a_all[0:C, 0:C2], 0.0).astype(BF16)
            d["a_o"] = jnp.concatenate([jnp.where(incl, a_all[C:, 0:C2], 0.0),
                                        -jnp.where(incl, a_all[C:, C2:], 0.0)], axis=1).astype(BF16)
            d["xm"] = -jnp.where(strict, a_all[0:C, C2:], 0.0)

    def st_p1(g):
        for d in g["chains"]:
            pw = d["xm"].astype(BF16)
            d["pf"] = _dot(pw, stack(pw))

    def st_neumann(s):
        def stage(g):
            for d in g["chains"]:
                pf = d["pf"]
                pw = pf.astype(BF16)
                xs = stack(d["xm"].astype(BF16))
                if 2 * s >= C:
                    d["xm"] = d["xm"] + pf + _dot(pw, xs)
                else:
                    both = _dot(pw, jnp.concatenate([stack(pw), xs], axis=1))
                    d["xm"] = d["xm"] + pf + both[:, C2:]
                    d["pf"] = both[:, 0:C2]
        return stage

    def st_state_in(g):
        for p, d in enumerate(g["chains"]):
            d["av"] = _dot(d.pop("a_kk"), d["v_s"])
            d["xm"] = d["xm"].astype(BF16)
            d["st"] = state_ref[g["bi"] * n_pairs + p]
            d["w1r1"] = _dot_nt(d["ktrt"], d["st"].astype(BF16))

    def st_u(g):
        for d in g["chains"]:
            rhs = d["w1r1"][0:C] + d["av"]
            d["u"] = (rhs + _dot(d["xm"], stack(rhs.astype(BF16)))).astype(BF16)

    def st_out(g):
        outs = []
        for p, d in enumerate(g["chains"]):
            u = d["u"]
            outs.append(d["w1r1"][C:C2] + _dot(d["a_o"], jnp.concatenate([d["v_s"], stack(u)], axis=0)))
            upd = _dot_tn(jnp.concatenate([d["v"], u], axis=0), d["kebe"])
            state_ref[g["bi"] * n_pairs + p] = d["st"] * d["p_last"] + jnp.where(same_state, upd, 0.0)
        g["o"] = jnp.concatenate(outs, axis=1)

    def st_center(g):
        g["d"] = g["o"] - _group_sum(g["o"], g64_ref, True) * (1.0 / N_B)

    def st_norm(g):
        bi, d = g["bi"], g["d"]
        var = _group_sum(d * d, g64_ref, False) * (1.0 / N_B)
        y = d * lax.rsqrt(var + RWKV_LN_EPS) * lnw_ref[...] + lnb_ref[...] + bonus_ref[bi].astype(F32)
        o_ref[bi] = (y * g_ref[bi].astype(F32)).astype(BF16)

    stages = [st_prep, st_a, st_p1]
    s = 2
    while s < C:
        stages.append(st_neumann(s))
        s *= 2
    stages += [st_state_in, st_u, st_out, st_center, st_norm]
    groups = [dict(bi=bi) for bi in range(n_seq)]
    for tau in range(len(stages) + len(groups) - 1):
        for gi, g in enumerate(groups):
            if 0 <= tau - gi < len(stages):
                stages[tau - gi](g)


def _rwkv_scan(mix, logs, ln_w, ln_b, g64):
    B, S, _ = mix.shape
    W = W_B
    t = RWKV_C
    nb = RWKV_NB if B % RWKV_NB == 0 else 1
    tok = lambda j: pl.BlockSpec((nb, t, W), lambda b, c: (b, c, j))
    m0 = _M_B // W
    return pl.pallas_call(
        _rwkv_kernel,
        grid=(B // nb, S // t),
        in_specs=[tok(m0 + j) for j in range(5)] + [tok(0), tok(m0 + 5), tok(m0 + 6)]
                 + [_pick_spec(ln_w), _pick_spec(ln_b), _pick_spec(g64)],
        out_specs=tok(0),
        out_shape=jax.ShapeDtypeStruct((B, S, W), BF16),
        scratch_shapes=[pltpu.VMEM((nb * H_B // 2, LANES, LANES), F32)],
        compiler_params=_cparams(2),
        name="rwkv7_scan",
    )(mix, mix, mix, mix, mix, logs, mix, mix, ln_w[0], ln_b[0], g64[0])


def _hgrn_kernel(q_ref, k_ref, v_ref, lf_ref, gate_ref, nw_ref, o_ref, state_ref):
    C = HGRN_C
    c = HGRN_SUB
    n_chunks = q_ref.shape[1] // C

    @pl.when(pl.program_id(1) == 0)
    def _():
        state_ref[...] = jnp.zeros_like(state_ref)

    lane_c = lax.broadcasted_iota(jnp.int32, (1, c), 1)
    row_c = lax.broadcasted_iota(jnp.int32, (c, 1), 0)

    units = []
    for n in range(n_chunks):
        rows = slice(n * C, (n + 1) * C)
        b = _cumsum_rows(lf_ref[0, rows, :], C) * math.log2(math.e)
        b_end = b[C - 1:C]
        q = q_ref[0, rows, :].astype(F32)
        k = k_ref[0, rows, :].astype(F32)
        v_all = v_ref[0, rows, :]
        q_in = (q * jnp.exp2(b)).astype(BF16)
        k_end = (k * jnp.exp2(b_end - b)).astype(BF16)
        p_end = jnp.exp2(b_end)
        for h in range(H_C):
            sl = slice(h * LANES, (h + 1) * LANES)
            units.append(dict(b=b[:, sl], q=q[:, sl], k=k[:, sl], v=v_all[:, sl], q_in=q_in[:, sl],
                              k_end=k_end[:, sl], p_end=p_end[:, sl]))

    for d in units:
        d["cross"] = []
        for i in range(1, C // c):
            r0 = i * c
            b_ref_ = d["b"][r0 - 1:r0]
            qs = (d["q"][r0:r0 + c] * jnp.exp2(d["b"][r0:r0 + c] - b_ref_)).astype(BF16)
            ks = (d["k"][0:r0] * jnp.exp2(b_ref_ - d["b"][0:r0])).astype(BF16)
            d["cross"].append(_dot_nt(qs, ks).astype(BF16))
    for d in units:
        d["diag"] = []
        for i in range(C // c):
            r0 = i * c
            b_i = d["b"][r0:r0 + c]
            q_i = d["q"][r0:r0 + c]
            k_i = d["k"][r0:r0 + c]
            sc = jnp.zeros((c, c), F32)
            for j in range(c):
                e = jnp.exp2(b_i - b_i[j:j + 1])
                col = jnp.sum(q_i * e * k_i[j:j + 1], axis=-1, keepdims=True)
                sc = jnp.where(lane_c == j, col, sc)
            d["diag"].append(jnp.where(lane_c <= row_c, sc, 0.0).astype(BF16))
    for d in units:
        blocks = []
        for i in range(C // c):
            r0 = i * c
            o_i = _dot(d["diag"][i], d["v"][r0:r0 + c])
            if i > 0:
                o_i = o_i + _dot(d["cross"][i - 1], d["v"][0:r0])
            blocks.append(o_i)
        d["intra"] = jnp.concatenate(blocks, axis=0)

    state = [state_ref[h] for h in range(H_C)]
    o_chunks = []
    for n in range(n_chunks):
        us = units[n * H_C:(n + 1) * H_C]
        o_heads = [_dot_nt(d["q_in"], st.astype(BF16)) + d["intra"] for d, st in zip(us, state)]
        state = [st * d["p_end"] + _dot_tn(d["v"], d["k_end"]) for d, st in zip(us, state)]
        o_chunks.append(jnp.concatenate(
            [o_h * lax.rsqrt(jnp.mean(o_h * o_h, axis=-1, keepdims=True) + EPS) * nw_ref[...] for o_h in o_heads],
            axis=1))
    for h in range(H_C):
        state_ref[h] = state[h]
    o = jnp.concatenate(o_chunks, axis=0)
    o_ref[0] = (o * gate_ref[0].astype(F32)).astype(BF16)


def _hgrn_scan(mix, logs, norm_w):
    B, S, _ = mix.shape
    W = W_C
    t = min(HGRN_T, S)
    tok = lambda j: pl.BlockSpec((1, t, W), lambda b, c: (b, c, j))
    m0 = _M_C // W
    return pl.pallas_call(
        _hgrn_kernel,
        grid=(B, S // t),
        in_specs=[tok(m0), tok(m0 + 1), tok(m0 + 2), tok(W_B // W), tok(m0 + 3), _pick_spec(norm_w)],
        out_specs=tok(0),
        out_shape=jax.ShapeDtypeStruct((B, S, W), BF16),
        scratch_shapes=[pltpu.VMEM((H_C, LANES, LANES), F32)],
        compiler_params=_cparams(2),
        name="hgrn2_scan",
    )(mix, mix, mix, logs, mix, norm_w[0])


def _merge_kernel(h_ref, mod_ref, oa_ref, ob_ref, oc_ref, gates_ref, wa_ref, wb_ref, wc_ref, wo_ref, o_ref):
    D = h_ref.shape[2]
    gate = lambda j: _sigmoid(gates_ref[0, :, j * D:(j + 1) * D].astype(F32))
    z = (gate(0) * _dot(oa_ref[0], wa_ref[...])
         + gate(1) * _dot(ob_ref[0], wb_ref[...])
         + gate(2) * _dot(oc_ref[0], wc_ref[...]))
    y = _dot(z.astype(BF16), wo_ref[...])
    o_ref[0] = h_ref[0] + mod_ref[0][5:6] * y


def _merge(h, mod, l, oa, ob, oc, mix, wa, wb, wc, wo):
    B, S, D = h.shape
    tm = min(MERGE_TM, S)
    tok = lambda width: pl.BlockSpec((1, tm, width), lambda b, s: (b, s, 0))
    return pl.pallas_call(
        _merge_kernel,
        grid=(B, S // tm),
        in_specs=[tok(D), _mod_spec(mod, l), tok(W_A), tok(W_B), tok(W_C), tok(3 * D),
                  _pick_spec(wa), _pick_spec(wb), _pick_spec(wc), _pick_spec(wo)],
        out_specs=tok(D),
        out_shape=jax.ShapeDtypeStruct((B, S, D), F32),
        compiler_params=_cparams(2),
        name="merge",
    )(h, mod, oa, ob, oc, mix, wa[0], wb[0], wc[0], wo[0])


def _block_ones(width, group):
    i = jnp.arange(width) // group
    return (i[:, None] == i[None, :]).astype(BF16)


def kernel(x, c, positions, mod_w, mod_b, norm_w, ffn_w_gate, ffn_w_up, ffn_w_down, w_in, qk_norm_w, lambda_qk,
           subln_w, w_out_a, rwkv_mu, rwkv_w0, rwkv_w2, rwkv_a0, rwkv_a2, rwkv_g2, rwkv_k_k, rwkv_k_a, rwkv_r_k,
           rwkv_ln_w, rwkv_ln_b, w_out_b, hgrn_lower_bounds, hgrn_norm_w, w_out_c, w_out):
    B, S, D = x.shape
    L = mod_w.shape[0]
    lb = jnp.cumsum(jax.nn.softmax(hgrn_lower_bounds.astype(F32), axis=0), axis=0)
    lb = lb - lb[0]
    log_lb = jnp.log(lb)
    log1m_lb = jnp.log1p(-lb)

    mod = _modulation(c, mod_w, mod_b).reshape(L, B, N_MOD, D)
    cos, sin = _rope_tables(positions)

    pad = R_G_PAD - R_G
    vec = lambda a: a.astype(F32).reshape(L, 1, -1)
    nw4 = norm_w.reshape(L, 3, 1, D)
    wg, wu, wd = ffn_w_gate.astype(BF16), ffn_w_up.astype(BF16), ffn_w_down.astype(BF16)
    w_bf = w_in.astype(BF16)
    w_pad = jnp.concatenate([w_bf[:, :, :P_A + P_B], jnp.zeros((L, D, pad), BF16), w_bf[:, :, P_A + P_B:]], axis=2)
    g64 = _pick(_block_ones(2 * LANES, N_B))
    mix_params = [
        jnp.tile(qk_norm_w, (1, 1, W_A // DH_A)), None, jnp.pad(rwkv_mu, ((0, 0), (0, pad)))[:, None, :],
        vec(rwkv_w0), jnp.pad(rwkv_w2, ((0, 0), (0, R_A), (0, 0))).astype(BF16),
        vec(rwkv_a0), jnp.pad(rwkv_a2, ((0, 0), (R_W, 0), (0, 0))).astype(BF16),
        jnp.pad(rwkv_g2, ((0, 0), (0, pad), (0, 0))).astype(BF16),
        vec(rwkv_k_k), vec(rwkv_k_a), vec(rwkv_r_k), vec(log_lb), vec(log1m_lb)]
    ln_w, ln_b, subln, hnw = vec(rwkv_ln_w), vec(rwkv_ln_b), vec(subln_w), vec(hgrn_norm_w)
    wa, wb, wc, wo = (a.astype(BF16) for a in (w_out_a, w_out_b, w_out_c, w_out))

    h = x
    for l in range(L):
        ffn_w = lambda j: (_pick(wg, l, j), _pick(wu, l, j), _pick(wd, l, j))
        h = _ffn(h, mod, l, _pick(nw4, l, 0), *ffn_w(0), 0)
        params = [_pick(nw4, l, 1), _pick(w_pad, l)] + [g64 if a is None else _pick(a, l) for a in mix_params]
        mix, logs = _mixer_in(h, mod, l, cos, sin, params)
        oa = _diff_attention(mix, _pick(lambda_qk, l), _pick(subln, l), l)
        ob = _rwkv_scan(mix, logs, _pick(ln_w, l), _pick(ln_b, l), g64)
        oc = _hgrn_scan(mix, logs, _pick(hnw, l))
        h = _merge(h, mod, l, oa, ob, oc, mix, _pick(wa, l), _pick(wb, l), _pick(wc, l), _pick(wo, l))
        h = _ffn(h, mod, l, _pick(nw4, l, 2), *ffn_w(1), 6)
    return h
```

```python
import functools
import math

import jax
import jax.numpy as jnp
from jax import lax
from jax.experimental import pallas as pl
from jax.experimental.pallas import tpu as pltpu

F32 = jnp.float32
BF16 = jnp.bfloat16

H_A, DH_A = 4, 64
W_A = H_A * 2 * DH_A
ROPE_THETA = 10000.0
H_B, N_B = 8, 64
W_B = H_B * N_B
R_W, R_A, R_G = 64, 64, 160
RWKV_LN_EPS = 64e-5
H_C, DK_C = 4, 128
W_C = H_C * DK_C
N_MOD = 9
EPS = 1e-6
P_A = 3 * W_A
P_B = 3 * W_B + R_W + R_A + R_G
P_C = 4 * W_C

LANES = 128
R_G_PAD = 256
P_B_PAD = 3 * W_B + R_W + R_A + R_G_PAD
VMEM_LIMIT = 56 * 1024 * 1024

FFN_TM = 1024
MIX_TM = 256
ATT_T = 512
RWKV_C = 64
RWKV_NB = 4
HGRN_C = 64
HGRN_SUB = 8
MERGE_TM = 1024


def _cparams(n_axes):
    return pltpu.CompilerParams(dimension_semantics=("arbitrary",) * n_axes,
                                vmem_limit_bytes=VMEM_LIMIT)


def _pick(arr, *lead):
    return arr, lead


def _pick_spec(p):
    arr, lead = p
    rest = arr.shape[len(lead):]
    index = tuple(lead) + (0,) * len(rest)
    return pl.BlockSpec((None,) * len(lead) + tuple(rest), lambda *_: index, pipeline_mode=pl.Buffered(1))


def _mod_spec(mod, l):
    return pl.BlockSpec((None, 1) + mod.shape[2:], lambda b, s: (l, b, 0, 0))


def _sigmoid(x):
    return 1.0 / (1.0 + jnp.exp(-x))


def _silu(x):
    return x * _sigmoid(x)


def _log1p_exp_neg_abs(x):
    return jnp.log(1.0 + jnp.exp(-jnp.abs(x)))


def _dot(a, b):
    return jnp.dot(a, b, preferred_element_type=F32)


def _dot_nt(a, b):
    return lax.dot_general(a, b, (((1,), (1,)), ((), ())), preferred_element_type=F32)


def _dot_tn(a, b):
    return lax.dot_general(a, b, (((0,), (0,)), ((), ())), preferred_element_type=F32)


def _group_sum(x, g_ref, split):
    g = g_ref[...]
    w = g.shape[0]

    def halves(xb):
        return jnp.concatenate([_dot(xb[:, i * w:(i + 1) * w], g) for i in range(x.shape[1] // w)], axis=1)

    hi = x.astype(BF16)
    if not split:
        return halves(hi)
    return halves(hi) + halves((x - hi.astype(F32)).astype(BF16))


def _norm_mod(x, nw, shift, scale):
    ms = jnp.mean(x * x, axis=-1, keepdims=True)
    return (x * lax.rsqrt(ms + EPS) * nw) * (1.0 + scale) + shift


def _cumsum_rows(x, n):
    row = lax.broadcasted_iota(jnp.int32, (n, 1), 0)
    s = 1
    while s < n:
        x = x + jnp.where(row >= s, pltpu.roll(x, s, 0), 0.0)
        s *= 2
    return x


def _mod_kernel(c_ref, w_ref, b_ref, o_ref):
    c = c_ref[...]
    cond = _silu(c)
    o_ref[0] = jnp.dot(cond, w_ref[0], preferred_element_type=F32,
                       precision=lax.Precision.HIGHEST) + b_ref[0]


def _modulation(c, mod_w, mod_b):
    L, D, N = mod_w.shape
    B = c.shape[0]
    tn = 1024
    return pl.pallas_call(
        _mod_kernel,
        grid=(L, N // tn),
        in_specs=[pl.BlockSpec((B, D), lambda l, j: (0, 0)),
                  pl.BlockSpec((1, D, tn), lambda l, j: (l, 0, j)),
                  pl.BlockSpec((1, 1, tn), lambda l, j: (l, 0, j))],
        out_specs=pl.BlockSpec((1, B, tn), lambda l, j: (l, 0, j)),
        out_shape=jax.ShapeDtypeStruct((L, B, N), F32),
        compiler_params=_cparams(2),
        name="modulation",
    )(c, mod_w, mod_b.reshape(L, 1, N))


def _rope_kernel(pos_ref, inv_ref, sign_ref, cos_ref, sin_ref):
    ang = pos_ref[0] * inv_ref[...]
    cos_ref[0] = jnp.cos(ang)
    sin_ref[0] = jnp.sin(ang) * sign_ref[...]


def _rope_tables(positions):
    B, S = positions.shape
    ts = 512 if S % 512 == 0 else S
    half = DH_A // 2
    inv = ROPE_THETA ** (-jnp.arange(0, DH_A, 2, dtype=F32) / DH_A)
    inv128 = jnp.tile(inv, LANES // half)[None, :]
    sign128 = jnp.tile(jnp.concatenate([-jnp.ones(half, F32), jnp.ones(half, F32)]), LANES // DH_A)[None, :]
    pos = positions.astype(F32)[..., None]
    return pl.pallas_call(
        _rope_kernel,
        grid=(B, S // ts),
        in_specs=[pl.BlockSpec((1, ts, 1), lambda b, s: (b, s, 0)),
                  pl.BlockSpec((1, LANES), lambda b, s: (0, 0)),
                  pl.BlockSpec((1, LANES), lambda b, s: (0, 0))],
        out_specs=[pl.BlockSpec((1, ts, LANES), lambda b, s: (b, s, 0))] * 2,
        out_shape=[jax.ShapeDtypeStruct((B, S, LANES), F32)] * 2,
        compiler_params=_cparams(2),
        name="rope_tables",
    )(pos, inv128, sign128)


def _ffn_kernel(chunks, shift_i, h_ref, mod_ref, nw_ref, wg_ref, wu_ref, wd_ref, o_ref, act_ref):
    x = h_ref[0]
    m = mod_ref[0]
    u = _norm_mod(x, nw_ref[...], m[shift_i:shift_i + 1], m[shift_i + 1:shift_i + 2]).astype(BF16)
    for lo, hi in chunks:
        g = _dot(u, wg_ref[:, lo:hi])
        up = _dot(u, wu_ref[:, lo:hi])
        act_ref[:, lo:hi] = (_silu(g) * up).astype(BF16)
    y = _dot(act_ref[...], wd_ref[...])
    o_ref[0] = x + (0.5 * m[shift_i + 2:shift_i + 3]) * y


def _ffn(h, mod, l, nw, wg, wu, wd, shift_i):
    B, S, D = h.shape
    F = wg[0].shape[-1]
    tm = min(FFN_TM, S)
    chunks, lo = [], 0
    while lo < F:
        hi = min(lo + 512, F)
        chunks.append((lo, hi))
        lo = hi
    return pl.pallas_call(
        functools.partial(_ffn_kernel, tuple(chunks), shift_i),
        grid=(B, S // tm),
        in_specs=[pl.BlockSpec((1, tm, D), lambda b, s: (b, s, 0)),
                  _mod_spec(mod, l), _pick_spec(nw), _pick_spec(wg), _pick_spec(wu), _pick_spec(wd)],
        out_specs=pl.BlockSpec((1, tm, D), lambda b, s: (b, s, 0)),
        out_shape=jax.ShapeDtypeStruct((B, S, D), F32),
        scratch_shapes=[pltpu.VMEM((tm, F), BF16)],
        compiler_params=_cparams(2),
        name="ffn",
    )(h, mod, nw[0], wg[0], wu[0], wd[0])


_O_QA, _O_KA, _O_VA = 0, W_A, 2 * W_A
_O_B = P_A
_O_C = P_A + P_B_PAD
_O_G = _O_C + P_C


_M_GATES = 0
_M_A = 3 * 1024
_M_B = _M_A + P_A
_M_TOTAL = _M_B + 7 * W_B


def _mixin_kernel(h_ref, mod_ref, nw_ref, w_ref, wcg_ref, cos_ref, sin_ref, qkw_ref, g64_ref, mu_ref,
                  w0_ref, w2_ref, a0_ref, a2_ref, g2_ref, kk_ref, ka_ref, rk_ref, llb_ref, l1lb_ref, hnw_ref,
                  mix_o, logs_o, oc_o, prev_ref, hstate_ref):
    tm = h_ref.shape[1]
    D = h_ref.shape[2]
    x = h_ref[0]
    m = mod_ref[0]
    u = _norm_mod(x, nw_ref[...], m[3:4], m[4:5]).astype(BF16)

    def put(off, val):
        mix_o[0, :, off:off + val.shape[1]] = val.astype(BF16)

    def qk_prep(p, wrow, scale):
        cos4 = jnp.concatenate([cos_ref[0]] * (W_A // LANES), axis=1)
        sin4 = jnp.concatenate([sin_ref[0]] * (W_A // LANES), axis=1)
        lane = lax.broadcasted_iota(jnp.int32, (1, W_A), 1)
        first_half = (lane % DH_A) < (DH_A // 2)
        ss = _group_sum(p * p, g64_ref, False)
        n = p * lax.rsqrt(ss * (1.0 / DH_A) + EPS) * wrow
        rot = jnp.where(first_half, pltpu.roll(n, W_A - DH_A // 2, 1), pltpu.roll(n, DH_A // 2, 1))
        return (n * cos4 + rot * sin4) * scale

    def use_qa(p):
        put(_M_A, qk_prep(p, qkw_ref[0:1], DH_A ** -0.5 * math.log2(math.e)))

    def use_ka(p):
        put(_M_A + W_A, qk_prep(p, qkw_ref[1:2], 1.0))

    rw = {}

    @pl.when(pl.program_id(1) == 0)
    def _():
        prev_ref[...] = jnp.zeros_like(prev_ref)
        hstate_ref[...] = jnp.zeros_like(hstate_ref)

    def shift_lerp(pb, lo):
        cols = slice(lo, lo + pb.shape[1])
        row = lax.broadcasted_iota(jnp.int32, (tm, 1), 0)
        shifted = jnp.where(row == 0, prev_ref[:, cols], pltpu.roll(pb, 1, 0))
        prev_ref[:, cols] = pb[tm - 1:tm]
        return pb + (shifted - pb) * mu_ref[:, cols]

    def use_rwkv_r(pb):
        rw["r"] = shift_lerp(pb, 0)
        put(_M_B, rw["r"])

    def use_rwkv_k(pb):
        rw["k"] = shift_lerp(pb, W_B)

    def use_rwkv_v(pb):
        rw["v"] = shift_lerp(pb, 2 * W_B)
        put(_M_B + 2 * W_B, rw["v"])

    def use_rwkv_loras(pb):
        x = shift_lerp(pb, 3 * W_B)
        xwa = x[:, 0:R_W + R_A]
        rw["xg"] = x[:, R_W + R_A:]
        z = w0_ref[...] + _dot(jnp.tanh(xwa).astype(BF16), w2_ref[...])
        logs_o[0, :, 0:W_B] = (-math.exp(-0.5)) * _sigmoid(z)
        rw["a"] = _sigmoid(a0_ref[...] + _dot(xwa.astype(BF16), a2_ref[...]))
        put(_M_B + 5 * W_B, _dot(_sigmoid(rw["xg"]).astype(BF16), g2_ref[...]))

    def use_rwkv_keys():
        k, a = rw["k"], rw["a"]
        kk = k * kk_ref[...]
        kkn = kk / jnp.maximum(jnp.sqrt(_group_sum(kk * kk, g64_ref, False)), 1e-12)
        rw["k2"] = k * (1.0 + (a - 1.0) * ka_ref[...])
        put(_M_B + W_B, rw["k2"])
        put(_M_B + 3 * W_B, kkn)
        put(_M_B + 4 * W_B, kkn * a)

    def use_rwkv_bonus():
        put(_M_B + 6 * W_B, _group_sum(rw["r"] * rw["k2"] * rk_ref[...], g64_ref, False) * rw["v"])

    hg = {}

    def use_qc(p):
        hg["q"] = _silu(p)

    def use_fz(fz):
        log_sig = jnp.minimum(fz, 0.0) - _log1p_exp_neg_abs(fz)
        t1 = llb_ref[...]
        t2 = l1lb_ref[...] + log_sig
        hg["lf"] = jnp.maximum(t1, t2) + _log1p_exp_neg_abs(t1 - t2)
        hg["k"] = 1.0 - jnp.exp(hg["lf"])

    def use_ic(p):
        hg["v"] = p.astype(BF16)

    def use_gc(p):
        hg["gate"] = _silu(p)

    piece = 2 * LANES
    pieces = [(_O_G + j, _M_GATES + j) for j in range(0, 3 * D, piece)]
    pieces += [(_O_VA + j, _M_A + 2 * W_A + j) for j in range(0, W_A, piece)]

    def next_piece():
        if pieces:
            lo, dst = pieces.pop(0)
            src, off = (w_ref, lo) if lo < _O_C else (wcg_ref, lo - _O_C)
            put(dst, _dot(u, src[:, off:off + piece]))

    hgrn_prepare, hgrn_diag, hgrn_finish = _hgrn_stages(hg, hnw_ref, hstate_ref, next_piece)

    seg = {"qa": (_O_QA, W_A), "ka": (_O_KA, W_A),
           "br": (_O_B, W_B), "bk": (_O_B + W_B, W_B), "bv": (_O_B + 2 * W_B, W_B),
           "bx": (_O_B + 3 * W_B, P_B_PAD - 3 * W_B),
           "qc": (_O_C, W_C), "fz": (_O_C + W_C, W_C), "ic": (_O_C + 2 * W_C, W_C), "gc": (_O_C + 3 * W_C, W_C)}
    p = {}

    def proj(name):
        lo, width = seg[name]
        src, off = (w_ref, lo) if lo < _O_C else (wcg_ref, lo - _O_C)
        p[name] = _dot(u, src[:, off:off + width])

    proj("qa")
    proj("ka")
    use_qa(p.pop("qa"))
    proj("bx")
    use_ka(p.pop("ka"))
    proj("bk")
    use_rwkv_loras(p.pop("bx"))
    proj("br")
    use_rwkv_k(p.pop("bk"))
    use_rwkv_keys()
    proj("bv")
    use_rwkv_r(p.pop("br"))
    proj("fz")
    use_rwkv_v(p.pop("bv"))
    use_rwkv_bonus()
    proj("qc")
    use_fz(p.pop("fz"))
    proj("gc")
    use_qc(p.pop("qc"))
    proj("ic")
    use_gc(p.pop("gc"))
    next_piece()
    use_ic(p.pop("ic"))
    hgrn_prepare()
    hgrn_diag()
    oc_o[0] = hgrn_finish()
    while pieces:
        next_piece()


def _mixer_in(h, mod, l, cos, sin, params):
    B, S, D = h.shape
    tm = min(MIX_TM, S)
    tok = lambda width: pl.BlockSpec((1, tm, width), lambda b, s: (b, s, 0))
    return pl.pallas_call(
        _mixin_kernel,
        grid=(B, S // tm),
        in_specs=[tok(D), _mod_spec(mod, l)] + [_pick_spec(p) for p in params[:3]] + [tok(LANES), tok(LANES)]
                 + [_pick_spec(p) for p in params[3:]],
        out_specs=[tok(_M_TOTAL), tok(W_B), tok(W_C)],
        out_shape=[jax.ShapeDtypeStruct((B, S, _M_TOTAL), BF16), jax.ShapeDtypeStruct((B, S, W_B), F32),
                   jax.ShapeDtypeStruct((B, S, W_C), BF16)],
        scratch_shapes=[pltpu.VMEM((1, P_B_PAD), F32), pltpu.VMEM((H_C, LANES, LANES), F32)],
        compiler_params=_cparams(2),
        name="mixer_in",
    )(h, mod, *[p[0] for p in params[:3]], cos, sin, *[p[0] for p in params[3:]])


def _attn_kernel(lam_init, t, q_ref, k_ref, v_ref, lq_ref, sw_ref, o_ref, qs_ref, m_ref, l_ref, acc_ref):
    n = q_ref.shape[1] // t
    lane = lax.broadcasted_iota(jnp.int32, (1, LANES), 1)
    for i in range(n):
        q = q_ref[0, i * t:(i + 1) * t, :]
        zero = jnp.zeros_like(q)
        qs_ref[i, 0:t] = jnp.where(lane < DH_A, q, zero)
        qs_ref[i, t:2 * t] = jnp.where(lane < DH_A, zero, q)
    m_ref[...] = jnp.full_like(m_ref, -jnp.inf)
    l_ref[...] = jnp.zeros_like(l_ref)
    acc_ref[...] = jnp.zeros_like(acc_ref)
    lq = lq_ref[...]
    lam = (jnp.exp(jnp.sum(lq[0:1] * lq[1:2], axis=-1, keepdims=True))
           - jnp.exp(jnp.sum(lq[2:3] * lq[3:4], axis=-1, keepdims=True)) + lam_init)

    def scores(i, j):
        return _dot_nt(qs_ref[i], k_ref[0, j * t:(j + 1) * t, :])

    def update(i, j, s):
        if i == j:
            r_i = lax.broadcasted_iota(jnp.int32, (2 * t, t), 0) % t
            c_i = lax.broadcasted_iota(jnp.int32, (2 * t, t), 1)
            s = jnp.where(c_i <= r_i, s, -jnp.inf)
        m_old = m_ref[i]
        m_new = jnp.maximum(m_old, jnp.max(s, axis=-1, keepdims=True))
        alpha = jnp.exp2(m_old - m_new)
        p = jnp.exp2(s - jnp.concatenate([m_new] * (t // LANES), axis=1))
        l_ref[i] = alpha * l_ref[i] + jnp.sum(p, axis=-1, keepdims=True)
        acc_ref[i] = alpha * acc_ref[i] + _dot(p.astype(BF16), v_ref[0, j * t:(j + 1) * t, :])
        m_ref[i] = m_new
        if i == j:
            o = acc_ref[i] / l_ref[i]
            o = o[0:t] - lam * o[t:2 * t]
            ms = jnp.mean(o * o, axis=-1, keepdims=True)
            o_ref[0, i * t:(i + 1) * t, :] = (o * lax.rsqrt(ms + EPS) * sw_ref[...] * (1.0 - lam_init)).astype(BF16)

    chains = [[], []]
    for a in range((n + 1) // 2):
        tiles = [a] if n - 1 - a == a else [a, n - 1 - a]
        shorter = chains[0] if len(chains[0]) <= len(chains[1]) else chains[1]
        shorter += [(i, j) for i in tiles for j in range(i + 1)]
    chain_a, chain_b = chains
    order = []
    for x in range(max(len(chain_a), len(chain_b))):
        order += chain_a[x:x + 1] + chain_b[x:x + 1]
    s_next = scores(*order[0])
    for x, (i, j) in enumerate(order):
        s_cur = s_next
        if x + 1 < len(order):
            s_next = scores(*order[x + 1])
        update(i, j, s_cur)


def _diff_attention(mix, lambda_qk, subln_w, layer_idx):
    B, S, _ = mix.shape
    t = min(ATT_T, S)
    n = S // t
    lam_init = 0.8 - 0.6 * math.exp(-0.3 * layer_idx)
    blk = lambda off: off // LANES
    spec = lambda off: pl.BlockSpec((1, S, LANES), lambda b, h: (b, 0, blk(off) + h))
    return pl.pallas_call(
        functools.partial(_attn_kernel, lam_init, t),
        grid=(B, H_A),
        in_specs=[spec(_M_A), spec(_M_A + W_A), spec(_M_A + 2 * W_A),
                  _pick_spec(lambda_qk), _pick_spec(subln_w)],
        out_specs=pl.BlockSpec((1, S, LANES), lambda b, h: (b, 0, h)),
        out_shape=jax.ShapeDtypeStruct((B, S, W_A), BF16),
        scratch_shapes=[pltpu.VMEM((n, 2 * t, LANES), BF16), pltpu.VMEM((n, 2 * t, LANES), F32),
                        pltpu.VMEM((n, 2 * t, LANES), F32), pltpu.VMEM((n, 2 * t, LANES), F32)],
        compiler_params=_cparams(2),
        name="diff_attention",
    )(mix, mix, mix, lambda_qk[0], subln_w[0])


def _rwkv_kernel(r_ref, k_ref, v_ref, kk_ref, beta_ref, lw_ref, g_ref, bonus_ref, lnw_ref, lnb_ref, g64_ref,
                 o_ref, state_ref):
    C = RWKV_C
    C2 = 2 * C

    @pl.when(pl.program_id(1) == 0)
    def _():
        state_ref[...] = jnp.zeros_like(state_ref)

    lane = lax.broadcasted_iota(jnp.int32, (1, LANES), 1)
    head0 = lane < N_B
    ri = lax.broadcasted_iota(jnp.int32, (C, 1), 0)
    ci = lax.broadcasted_iota(jnp.int32, (1, C2), 1) % C
    strict = ci < ri
    incl = ci <= ri
    same_state = (lax.broadcasted_iota(jnp.int32, (LANES, 1), 0) < N_B) == head0

    def stack(x):
        z = jnp.zeros_like(x)
        return jnp.concatenate([jnp.where(head0, x, z), jnp.where(head0, z, x)], axis=0)

    n_pairs = H_B // 2
    n_seq = r_ref.shape[0]

    def st_prep(g):
        bi = g["bi"]
        lw = lw_ref[bi]
        cum = _cumsum_rows(lw, C)
        c_last = cum[C - 1:C]
        p_inv = jnp.exp(-cum)
        p_end = jnp.exp(c_last - cum)
        p_last = jnp.exp(c_last)
        k = k_ref[bi].astype(F32)
        beta = beta_ref[bi].astype(F32)
        v_all = v_ref[bi]
        rt = (r_ref[bi].astype(F32) * jnp.exp(cum)).astype(BF16)
        kt = (kk_ref[bi].astype(F32) * jnp.exp(cum - lw)).astype(BF16)
        kh = (k * p_inv).astype(BF16)
        bh = (beta * p_inv).astype(BF16)
        ke = (k * p_end).astype(BF16)
        be = (beta * p_end).astype(BF16)
        g["chains"] = []
        for p in range(n_pairs):
            sl = slice(p * LANES, (p + 1) * LANES)
            v = v_all[:, sl]
            g["chains"].append(dict(
                ktrt=jnp.concatenate([kt[:, sl], rt[:, sl]], axis=0),
                rhs4=jnp.concatenate([stack(kh[:, sl]), stack(bh[:, sl])], axis=0),
                v=v, v_s=stack(v),
                kebe=jnp.concatenate([ke[:, sl], -be[:, sl]], axis=0),
                p_last=p_last[:, sl]))

    def st_a(g):
        for d in g["chains"]:
            a_all = _dot_nt(d["ktrt"], d.pop("rhs4"))
            d["a_kk"] = jnp.where(strict, a_all[0:C, 0:C2], 0.0).astype(BF16)
            d["a_o"] = jnp.concatenate([jnp.where(incl, a_all[C:, 0:C2], 0.0),
                                        -jnp.where(incl, a_all[C:, C2:], 0.0)], axis=1).astype(BF16)
            d["xm"] = -jnp.where(strict, a_all[0:C, C2:], 0.0)

    def st_p1(g):
        for d in g["chains"]:
            pw = d["xm"].astype(BF16)
            d["pf"] = _dot(pw, stack(pw))

    def st_neumann(s):
        def stage(g):
            for d in g["chains"]:
                pf = d["pf"]
                pw = pf.astype(BF16)
                xs = stack(d["xm"].astype(BF16))
                if 2 * s >= C:
                    d["xm"] = d["xm"] + pf + _dot(pw, xs)
                else:
                    both = _dot(pw, jnp.concatenate([stack(pw), xs], axis=1))
                    d["xm"] = d["xm"] + pf + both[:, C2:]
                    d["pf"] = both[:, 0:C2]
        return stage

    def st_state_in(g):
        for p, d in enumerate(g["chains"]):
            d["av"] = _dot(d.pop("a_kk"), d["v_s"])
            d["xm"] = d["xm"].astype(BF16)
            d["st"] = state_ref[g["bi"] * n_pairs + p]
            d["w1r1"] = _dot_nt(d["ktrt"], d["st"].astype(BF16))

    def st_u(g):
        for d in g["chains"]:
            rhs = d["w1r1"][0:C] + d["av"]
            d["u"] = (rhs + _dot(d["xm"], stack(rhs.astype(BF16)))).astype(BF16)

    def st_out(g):
        outs = []
        for p, d in enumerate(g["chains"]):
            u = d["u"]
            outs.append(d["w1r1"][C:C2] + _dot(d["a_o"], jnp.concatenate([d["v_s"], stack(u)], axis=0)))
            upd = _dot_tn(jnp.concatenate([d["v"], u], axis=0), d["kebe"])
            state_ref[g["bi"] * n_pairs + p] = d["st"] * d["p_last"] + jnp.where(same_state, upd, 0.0)
        g["o"] = jnp.concatenate(outs, axis=1)

    def st_center(g):
        g["d"] = g["o"] - _group_sum(g["o"], g64_ref, True) * (1.0 / N_B)

    def st_norm(g):
        bi, d = g["bi"], g["d"]
        var = _group_sum(d * d, g64_ref, False) * (1.0 / N_B)
        y = d * lax.rsqrt(var + RWKV_LN_EPS) * lnw_ref[...] + lnb_ref[...] + bonus_ref[bi].astype(F32)
        o_ref[bi] = (y * g_ref[bi].astype(F32)).astype(BF16)

    stages = [st_prep, st_a, st_p1]
    s = 2
    while s < C:
        stages.append(st_neumann(s))
        s *= 2
    stages += [st_state_in, st_u, st_out, st_center, st_norm]
    groups = [dict(bi=bi) for bi in range(n_seq)]
    for tau in range(len(stages) + len(groups) - 1):
        for gi, g in enumerate(groups):
            if 0 <= tau - gi < len(stages):
                stages[tau - gi](g)


def _rwkv_scan(mix, logs, ln_w, ln_b, g64):
    B, S, _ = mix.shape
    W = W_B
    t = RWKV_C
    nb = RWKV_NB if B % RWKV_NB == 0 else 1
    tok = lambda j: pl.BlockSpec((nb, t, W), lambda b, c: (b, c, j))
    m0 = _M_B // W
    return pl.pallas_call(
        _rwkv_kernel,
        grid=(B // nb, S // t),
        in_specs=[tok(m0 + j) for j in range(5)] + [tok(0), tok(m0 + 5), tok(m0 + 6)]
                 + [_pick_spec(ln_w), _pick_spec(ln_b), _pick_spec(g64)],
        out_specs=tok(0),
        out_shape=jax.ShapeDtypeStruct((B, S, W), BF16),
        scratch_shapes=[pltpu.VMEM((nb * H_B // 2, LANES, LANES), F32)],
        compiler_params=_cparams(2),
        name="rwkv7_scan",
    )(mix, mix, mix, mix, mix, logs, mix, mix, ln_w[0], ln_b[0], g64[0])


def _hgrn_stages(hg, nw_ref, state_ref, filler):
    C = HGRN_C
    c = HGRN_SUB
    lane_c = lax.broadcasted_iota(jnp.int32, (1, c), 1)
    row_c = lax.broadcasted_iota(jnp.int32, (c, 1), 0)
    units = []

    def prepare():
        q, k, v_all, lf = hg["q"], hg["k"], hg["v"], hg["lf"]
        for n in range(q.shape[0] // C):
            rows = slice(n * C, (n + 1) * C)
            b = _cumsum_rows(lf[rows], C) * math.log2(math.e)
            b_end = b[C - 1:C]
            q_in = (q[rows] * jnp.exp2(b)).astype(BF16)
            k_end = (k[rows] * jnp.exp2(b_end - b)).astype(BF16)
            p_end = jnp.exp2(b_end)
            for h in range(H_C):
                sl = slice(h * LANES, (h + 1) * LANES)
                units.append(dict(b=b[:, sl], q=q[rows, sl], k=k[rows, sl], v=v_all[rows, sl], q_in=q_in[:, sl],
                                  k_end=k_end[:, sl], p_end=p_end[:, sl]))
        for d in units:
            d["cross"] = []
            for i in range(1, C // c):
                r0 = i * c
                b_ref_ = d["b"][r0 - 1:r0]
                qs = (d["q"][r0:r0 + c] * jnp.exp2(d["b"][r0:r0 + c] - b_ref_)).astype(BF16)
                ks = (d["k"][0:r0] * jnp.exp2(b_ref_ - d["b"][0:r0])).astype(BF16)
                d["cross"].append(_dot_nt(qs, ks).astype(BF16))

    def diag():
        for d in units:
            d["diag"] = []
            for i in range(C // c):
                r0 = i * c
                b_i = d["b"][r0:r0 + c]
                q_i = d["q"][r0:r0 + c]
                k_i = d["k"][r0:r0 + c]
                sc = jnp.zeros((c, c), F32)
                for j in range(c):
                    e = jnp.exp2(b_i - b_i[j:j + 1])
                    col = jnp.sum(q_i * e * k_i[j:j + 1], axis=-1, keepdims=True)
                    sc = jnp.where(lane_c == j, col, sc)
                d["diag"].append(jnp.where(lane_c <= row_c, sc, 0.0).astype(BF16))
            filler()

    def finish():
        for d in units:
            blocks = []
            for i in range(C // c):
                r0 = i * c
                o_i = _dot(d["diag"][i], d["v"][r0:r0 + c])
                if i > 0:
                    o_i = o_i + _dot(d["cross"][i - 1], d["v"][0:r0])
                blocks.append(o_i)
            d["intra"] = jnp.concatenate(blocks, axis=0)
        state = [state_ref[h] for h in range(H_C)]
        o_chunks = []
        for n in range(len(units) // H_C):
            us = units[n * H_C:(n + 1) * H_C]
            o_heads = [_dot_nt(d["q_in"], st.astype(BF16)) + d["intra"] for d, st in zip(us, state)]
            state = [st * d["p_end"] + _dot_tn(d["v"], d["k_end"]) for d, st in zip(us, state)]
            o_chunks.append(jnp.concatenate(
                [o_h * lax.rsqrt(jnp.mean(o_h * o_h, axis=-1, keepdims=True) + EPS) * nw_ref[...] for o_h in o_heads],
                axis=1))
        for h in range(H_C):
            state_ref[h] = state[h]
        return (jnp.concatenate(o_chunks, axis=0) * hg["gate"]).astype(BF16)

    return [prepare, diag, finish]


def _merge_kernel(h_ref, mod_ref, oa_ref, ob_ref, oc_ref, gates_ref, wa_ref, wb_ref, wc_ref, wo_ref, o_ref):
    D = h_ref.shape[2]
    gate = lambda j: _sigmoid(gates_ref[0, :, j * D:(j + 1) * D].astype(F32))
    z = (gate(0) * _dot(oa_ref[0], wa_ref[...])
         + gate(1) * _dot(ob_ref[0], wb_ref[...])
         + gate(2) * _dot(oc_ref[0], wc_ref[...]))
    y = _dot(z.astype(BF16), wo_ref[...])
    o_ref[0] = h_ref[0] + mod_ref[0][5:6] * y


def _merge(h, mod, l, oa, ob, oc, mix, wa, wb, wc, wo):
    B, S, D = h.shape
    tm = min(MERGE_TM, S)
    tok = lambda width: pl.BlockSpec((1, tm, width), lambda b, s: (b, s, 0))
    return pl.pallas_call(
        _merge_kernel,
        grid=(B, S // tm),
        in_specs=[tok(D), _mod_spec(mod, l), tok(W_A), tok(W_B), tok(W_C), tok(3 * D),
                  _pick_spec(wa), _pick_spec(wb), _pick_spec(wc), _pick_spec(wo)],
        out_specs=tok(D),
        out_shape=jax.ShapeDtypeStruct((B, S, D), F32),
        compiler_params=_cparams(2),
        name="merge",
    )(h, mod, oa, ob, oc, mix, wa[0], wb[0], wc[0], wo[0])


def _block_ones(width, group):
    i = jnp.arange(width) // group
    return (i[:, None] == i[None, :]).astype(BF16)


def kernel(x, c, positions, mod_w, mod_b, norm_w, ffn_w_gate, ffn_w_up, ffn_w_down, w_in, qk_norm_w, lambda_qk,
           subln_w, w_out_a, rwkv_mu, rwkv_w0, rwkv_w2, rwkv_a0, rwkv_a2, rwkv_g2, rwkv_k_k, rwkv_k_a, rwkv_r_k,
           rwkv_ln_w, rwkv_ln_b, w_out_b, hgrn_lower_bounds, hgrn_norm_w, w_out_c, w_out):
    B, S, D = x.shape
    L = mod_w.shape[0]
    lb = jnp.cumsum(jax.nn.softmax(hgrn_lower_bounds.astype(F32), axis=0), axis=0)
    lb = lb - lb[0]
    log_lb = jnp.log(lb)
    log1m_lb = jnp.log1p(-lb)

    mod = _modulation(c, mod_w, mod_b).reshape(L, B, N_MOD, D)
    cos, sin = _rope_tables(positions)

    pad = R_G_PAD - R_G
    vec = lambda a: a.astype(F32).reshape(L, 1, -1)
    nw4 = norm_w.reshape(L, 3, 1, D)
    wg, wu, wd = ffn_w_gate.astype(BF16), ffn_w_up.astype(BF16), ffn_w_down.astype(BF16)
    w_ab = jnp.pad(w_in[:, :, :P_A + P_B].astype(BF16), ((0, 0), (0, 0), (0, pad)))
    w_cg = w_in[:, :, P_A + P_B:].astype(BF16)
    g64 = _pick(_block_ones(2 * LANES, N_B))
    mix_params = [
        jnp.tile(qk_norm_w, (1, 1, W_A // DH_A)), None, jnp.pad(rwkv_mu, ((0, 0), (0, pad)))[:, None, :],
        vec(rwkv_w0), jnp.pad(rwkv_w2, ((0, 0), (0, R_A), (0, 0))).astype(BF16),
        vec(rwkv_a0), jnp.pad(rwkv_a2, ((0, 0), (R_W, 0), (0, 0))).astype(BF16),
        jnp.pad(rwkv_g2, ((0, 0), (0, pad), (0, 0))).astype(BF16),
        vec(rwkv_k_k), vec(rwkv_k_a), vec(rwkv_r_k), vec(log_lb), vec(log1m_lb)]
    mix_params.append(vec(hgrn_norm_w))
    ln_w, ln_b, subln = vec(rwkv_ln_w), vec(rwkv_ln_b), vec(subln_w)
    wa, wb, wc, wo = (a.astype(BF16) for a in (w_out_a, w_out_b, w_out_c, w_out))

    h = x
    for l in range(L):
        ffn_w = lambda j: (_pick(wg, l, j), _pick(wu, l, j), _pick(wd, l, j))
        h = _ffn(h, mod, l, _pick(nw4, l, 0), *ffn_w(0), 0)
        params = ([_pick(nw4, l, 1), _pick(w_ab, l), _pick(w_cg, l)]
                  + [g64 if a is None else _pick(a, l) for a in mix_params])
        mix, logs, oc = _mixer_in(h, mod, l, cos, sin, params)
        oa = _diff_attention(mix, _pick(lambda_qk, l), _pick(subln, l), l)
        ob = _rwkv_scan(mix, logs, _pick(ln_w, l), _pick(ln_b, l), g64)
        h = _merge(h, mod, l, oa, ob, oc, mix, _pick(wa, l), _pick(wb, l), _pick(wc, l), _pick(wo, l))
        h = _ffn(h, mod, l, _pick(nw4, l, 2), *ffn_w(1), 6)
    return h
```

```python
import functools
import math

import jax
import jax.numpy as jnp
from jax import lax
from jax.experimental import pallas as pl
from jax.experimental.pallas import tpu as pltpu

F32 = jnp.float32
BF16 = jnp.bfloat16

H_A, DH_A = 4, 64
W_A = H_A * 2 * DH_A
ROPE_THETA = 10000.0
H_B, N_B = 8, 64
W_B = H_B * N_B
R_W, R_A, R_G = 64, 64, 160
RWKV_LN_EPS = 64e-5
H_C, DK_C = 4, 128
W_C = H_C * DK_C
N_MOD = 9
EPS = 1e-6
P_A = 3 * W_A
P_B = 3 * W_B + R_W + R_A + R_G
P_C = 4 * W_C

LANES = 128
R_G_PAD = 256
P_B_PAD = 3 * W_B + R_W + R_A + R_G_PAD
VMEM_LIMIT = 56 * 1024 * 1024

FFN_TM = 1024
MIX_TM = 512
ATT_T = 512
ATT_HEADS = 2
RWKV_C = 64
RWKV_NB = 16
HGRN_C = 64
HGRN_T = 1024
HGRN_SUB = 8
MERGE_TM = 1024


def _cparams(n_axes):
    return pltpu.CompilerParams(dimension_semantics=("arbitrary",) * n_axes,
                                vmem_limit_bytes=VMEM_LIMIT)


def _pick(arr, *lead):
    return arr, lead


def _pick_spec(p):
    arr, lead = p
    rest = arr.shape[len(lead):]
    index = tuple(lead) + (0,) * len(rest)
    return pl.BlockSpec((None,) * len(lead) + tuple(rest), lambda *_: index, pipeline_mode=pl.Buffered(1))


def _mod_spec(mod, l):
    return pl.BlockSpec((None, 1) + mod.shape[2:], lambda b, s: (l, b, 0, 0))


def _sigmoid(x):
    return 1.0 / (1.0 + jnp.exp(-x))


def _silu(x):
    return x * _sigmoid(x)


def _log1p_exp_neg_abs(x):
    return jnp.log(1.0 + jnp.exp(-jnp.abs(x)))


def _dot(a, b):
    return jnp.dot(a, b, preferred_element_type=F32)


def _dot_nt(a, b):
    return lax.dot_general(a, b, (((1,), (1,)), ((), ())), preferred_element_type=F32)


def _dot_tn(a, b):
    return lax.dot_general(a, b, (((0,), (0,)), ((), ())), preferred_element_type=F32)


def _group_sum(x, g_ref, split):
    g = g_ref[...]
    w = g.shape[0]

    def halves(xb):
        return jnp.concatenate([_dot(xb[:, i * w:(i + 1) * w], g) for i in range(x.shape[1] // w)], axis=1)

    hi = x.astype(BF16)
    if not split:
        return halves(hi)
    return halves(hi) + halves((x - hi.astype(F32)).astype(BF16))


def _norm_mod(x, nw, shift, scale):
    ms = jnp.mean(x * x, axis=-1, keepdims=True)
    return (x * lax.rsqrt(ms + EPS) * nw) * (1.0 + scale) + shift


def _cumsum_rows(x, n):
    row = lax.broadcasted_iota(jnp.int32, (n, 1), 0)
    s = 1
    while s < n:
        x = x + jnp.where(row >= s, pltpu.roll(x, s, 0), 0.0)
        s *= 2
    return x


def _mod_kernel(c_ref, w_ref, b_ref, o_ref):
    c = c_ref[...]
    cond = _silu(c)
    o_ref[0] = jnp.dot(cond, w_ref[0], preferred_element_type=F32,
                       precision=lax.Precision.HIGHEST) + b_ref[0]


def _modulation(c, mod_w, mod_b):
    L, D, N = mod_w.shape
    B = c.shape[0]
    tn = 1024
    return pl.pallas_call(
        _mod_kernel,
        grid=(L, N // tn),
        in_specs=[pl.BlockSpec((B, D), lambda l, j: (0, 0)),
                  pl.BlockSpec((1, D, tn), lambda l, j: (l, 0, j)),
                  pl.BlockSpec((1, 1, tn), lambda l, j: (l, 0, j))],
        out_specs=pl.BlockSpec((1, B, tn), lambda l, j: (l, 0, j)),
        out_shape=jax.ShapeDtypeStruct((L, B, N), F32),
        compiler_params=_cparams(2),
        name="modulation",
    )(c, mod_w, mod_b.reshape(L, 1, N))


def _rope_kernel(pos_ref, inv_ref, sign_ref, cos_ref, sin_ref):
    ang = pos_ref[0] * inv_ref[...]
    cos_ref[0] = jnp.cos(ang)
    sin_ref[0] = jnp.sin(ang) * sign_ref[...]


def _rope_tables(positions):
    B, S = positions.shape
    ts = 512 if S % 512 == 0 else S
    half = DH_A // 2
    inv = ROPE_THETA ** (-jnp.arange(0, DH_A, 2, dtype=F32) / DH_A)
    inv128 = jnp.tile(inv, LANES // half)[None, :]
    sign128 = jnp.tile(jnp.concatenate([-jnp.ones(half, F32), jnp.ones(half, F32)]), LANES // DH_A)[None, :]
    pos = positions.astype(F32)[..., None]
    return pl.pallas_call(
        _rope_kernel,
        grid=(B, S // ts),
        in_specs=[pl.BlockSpec((1, ts, 1), lambda b, s: (b, s, 0)),
                  pl.BlockSpec((1, LANES), lambda b, s: (0, 0)),
                  pl.BlockSpec((1, LANES), lambda b, s: (0, 0))],
        out_specs=[pl.BlockSpec((1, ts, LANES), lambda b, s: (b, s, 0))] * 2,
        out_shape=[jax.ShapeDtypeStruct((B, S, LANES), F32)] * 2,
        compiler_params=_cparams(2),
        name="rope_tables",
    )(pos, inv128, sign128)


def _ffn_kernel(chunks, shift_i, h_ref, mod_ref, nw_ref, wg_ref, wu_ref, wd_ref, o_ref, act_ref):
    x = h_ref[0]
    m = mod_ref[0]
    u = _norm_mod(x, nw_ref[...], m[shift_i:shift_i + 1], m[shift_i + 1:shift_i + 2]).astype(BF16)
    for lo, hi in chunks:
        g = _dot(u, wg_ref[:, lo:hi])
        up = _dot(u, wu_ref[:, lo:hi])
        act_ref[:, lo:hi] = (_silu(g) * up).astype(BF16)
    y = _dot(act_ref[...], wd_ref[...])
    o_ref[0] = x + (0.5 * m[shift_i + 2:shift_i + 3]) * y


def _ffn(h, mod, l, nw, wg, wu, wd, shift_i):
    B, S, D = h.shape
    F = wg[0].shape[-1]
    tm = min(FFN_TM, S)
    chunks, lo = [], 0
    while lo < F:
        hi = min(lo + 512, F)
        chunks.append((lo, hi))
        lo = hi
    return pl.pallas_call(
        functools.partial(_ffn_kernel, tuple(chunks), shift_i),
        grid=(B, S // tm),
        in_specs=[pl.BlockSpec((1, tm, D), lambda b, s: (b, s, 0)),
                  _mod_spec(mod, l), _pick_spec(nw), _pick_spec(wg), _pick_spec(wu), _pick_spec(wd)],
        out_specs=pl.BlockSpec((1, tm, D), lambda b, s: (b, s, 0)),
        out_shape=jax.ShapeDtypeStruct((B, S, D), F32),
        scratch_shapes=[pltpu.VMEM((tm, F), BF16)],
        compiler_params=_cparams(2),
        name="ffn",
    )(h, mod, nw[0], wg[0], wu[0], wd[0])


_O_QA, _O_KA, _O_VA = 0, W_A, 2 * W_A
_O_B = P_A
_O_C = P_A + P_B_PAD
_O_G = _O_C + P_C


_M_GATES = 0
_M_A = 3 * 1024
_M_B = _M_A + P_A
_M_C = _M_B + 7 * W_B
_M_TOTAL = _M_C + P_C


def _mixin_kernel(h_ref, mod_ref, nw_ref, w_ref, wcg_ref, cos_ref, sin_ref, qkw_ref, g64_ref, mu_ref,
                  w0_ref, w2_ref, a0_ref, a2_ref, g2_ref, kk_ref, ka_ref, rk_ref, llb_ref, l1lb_ref,
                  mix_o, logs_o, prev_ref):
    tm = h_ref.shape[1]
    D = h_ref.shape[2]
    x = h_ref[0]
    m = mod_ref[0]
    u = _norm_mod(x, nw_ref[...], m[3:4], m[4:5]).astype(BF16)

    def put(off, val):
        mix_o[0, :, off:off + val.shape[1]] = val.astype(BF16)

    def qk_prep(p, wrow, scale):
        cos4 = jnp.concatenate([cos_ref[0]] * (W_A // LANES), axis=1)
        sin4 = jnp.concatenate([sin_ref[0]] * (W_A // LANES), axis=1)
        lane = lax.broadcasted_iota(jnp.int32, (1, W_A), 1)
        first_half = (lane % DH_A) < (DH_A // 2)
        ss = _group_sum(p * p, g64_ref, False)
        n = p * lax.rsqrt(ss * (1.0 / DH_A) + EPS) * wrow
        rot = jnp.where(first_half, pltpu.roll(n, W_A - DH_A // 2, 1), pltpu.roll(n, DH_A // 2, 1))
        return (n * cos4 + rot * sin4) * scale

    def use_qa(p):
        put(_M_A, qk_prep(p, qkw_ref[0:1], DH_A ** -0.5 * math.log2(math.e)))

    def use_ka(p):
        put(_M_A + W_A, qk_prep(p, qkw_ref[1:2], 1.0))

    def use_va(p):
        put(_M_A + 2 * W_A, p)

    rw = {}

    @pl.when(pl.program_id(1) == 0)
    def _():
        prev_ref[...] = jnp.zeros_like(prev_ref)

    def shift_lerp(pb, lo):
        cols = slice(lo, lo + pb.shape[1])
        row = lax.broadcasted_iota(jnp.int32, (tm, 1), 0)
        shifted = jnp.where(row == 0, prev_ref[:, cols], pltpu.roll(pb, 1, 0))
        prev_ref[:, cols] = pb[tm - 1:tm]
        return pb + (shifted - pb) * mu_ref[:, cols]

    def use_rwkv_r(pb):
        rw["r"] = shift_lerp(pb, 0)
        put(_M_B, rw["r"])

    def use_rwkv_k(pb):
        rw["k"] = shift_lerp(pb, W_B)

    def use_rwkv_v(pb):
        rw["v"] = shift_lerp(pb, 2 * W_B)
        put(_M_B + 2 * W_B, rw["v"])

    def use_rwkv_loras(pb):
        x = shift_lerp(pb, 3 * W_B)
        xwa = x[:, 0:R_W + R_A]
        rw["xg"] = x[:, R_W + R_A:]
        z = w0_ref[...] + _dot(jnp.tanh(xwa).astype(BF16), w2_ref[...])
        logs_o[0, :, 0:W_B] = (-math.exp(-0.5)) * _sigmoid(z)
        rw["a"] = _sigmoid(a0_ref[...] + _dot(xwa.astype(BF16), a2_ref[...]))
        put(_M_B + 5 * W_B, _dot(_sigmoid(rw["xg"]).astype(BF16), g2_ref[...]))

    def use_rwkv_keys():
        k, a = rw["k"], rw["a"]
        kk = k * kk_ref[...]
        kkn = kk / jnp.maximum(jnp.sqrt(_group_sum(kk * kk, g64_ref, False)), 1e-12)
        rw["k2"] = k * (1.0 + (a - 1.0) * ka_ref[...])
        put(_M_B + W_B, rw["k2"])
        put(_M_B + 3 * W_B, kkn)
        put(_M_B + 4 * W_B, kkn * a)

    def use_rwkv_bonus():
        put(_M_B + 6 * W_B, _group_sum(rw["r"] * rw["k2"] * rk_ref[...], g64_ref, False) * rw["v"])

    def use_qc(p):
        put(_M_C, _silu(p))

    def use_fz(fz):
        log_sig = jnp.minimum(fz, 0.0) - _log1p_exp_neg_abs(fz)
        t1 = llb_ref[...]
        t2 = l1lb_ref[...] + log_sig
        logf = jnp.maximum(t1, t2) + _log1p_exp_neg_abs(t1 - t2)
        logs_o[0, :, W_B:W_B + W_C] = logf
        put(_M_C + W_C, 1.0 - jnp.exp(logf))

    def use_ic(p):
        put(_M_C + 2 * W_C, p)

    def use_gc(p):
        put(_M_C + 3 * W_C, _silu(p))

    def use_gates(j):
        return lambda p: put(_M_GATES + j * D, p)

    seg = {"qa": (_O_QA, W_A), "ka": (_O_KA, W_A), "va": (_O_VA, W_A),
           "br": (_O_B, W_B), "bk": (_O_B + W_B, W_B), "bv": (_O_B + 2 * W_B, W_B),
           "bx": (_O_B + 3 * W_B, P_B_PAD - 3 * W_B),
           "qc": (_O_C, W_C), "fz": (_O_C + W_C, W_C), "ic": (_O_C + 2 * W_C, W_C), "gc": (_O_C + 3 * W_C, W_C),
           "g0": (_O_G, D), "g1": (_O_G + D, D), "g2": (_O_G + 2 * D, D)}
    p = {}

    def proj(name):
        lo, width = seg[name]
        src, off = (w_ref, lo) if lo < _O_C else (wcg_ref, lo - _O_C)
        p[name] = _dot(u, src[:, off:off + width])

    proj("qa")
    proj("ka")
    use_qa(p.pop("qa"))
    proj("bx")
    use_ka(p.pop("ka"))
    proj("bk")
    use_rwkv_loras(p.pop("bx"))
    proj("br")
    use_rwkv_k(p.pop("bk"))
    use_rwkv_keys()
    proj("bv")
    use_rwkv_r(p.pop("br"))
    proj("fz")
    use_rwkv_v(p.pop("bv"))
    use_rwkv_bonus()
    proj("qc")
    use_fz(p.pop("fz"))
    proj("gc")
    use_qc(p.pop("qc"))
    proj("g0")
    use_gc(p.pop("gc"))
    proj("g1")
    use_gates(0)(p.pop("g0"))
    proj("g2")
    use_gates(1)(p.pop("g1"))
    proj("va")
    use_gates(2)(p.pop("g2"))
    proj("ic")
    use_va(p.pop("va"))
    use_ic(p.pop("ic"))


def _mixer_in(h, mod, l, cos, sin, params):
    B, S, D = h.shape
    tm = min(MIX_TM, S)
    tok = lambda width: pl.BlockSpec((1, tm, width), lambda b, s: (b, s, 0))
    return pl.pallas_call(
        _mixin_kernel,
        grid=(B, S // tm),
        in_specs=[tok(D), _mod_spec(mod, l)] + [_pick_spec(p) for p in params[:3]] + [tok(LANES), tok(LANES)]
                 + [_pick_spec(p) for p in params[3:]],
        out_specs=[tok(_M_TOTAL), tok(W_B + W_C)],
        out_shape=[jax.ShapeDtypeStruct((B, S, _M_TOTAL), BF16), jax.ShapeDtypeStruct((B, S, W_B + W_C), F32)],
        scratch_shapes=[pltpu.VMEM((1, P_B_PAD), F32)],
        compiler_params=_cparams(2),
        name="mixer_in",
    )(h, mod, *[p[0] for p in params[:3]], cos, sin, *[p[0] for p in params[3:]])


def _attn_kernel(lam_init, t, n_heads, q_ref, k_ref, v_ref, lq_ref, sw_ref, o_ref, qs_ref, m_ref, l_ref, acc_ref):
    n = q_ref.shape[1] // t
    lane = lax.broadcasted_iota(jnp.int32, (1, LANES), 1)
    lanes = lambda hh: slice(hh * LANES, (hh + 1) * LANES)
    for hh in range(n_heads):
        for i in range(n):
            q = q_ref[0, i * t:(i + 1) * t, lanes(hh)]
            zero = jnp.zeros_like(q)
            qs_ref[hh * n + i, 0:t] = jnp.where(lane < DH_A, q, zero)
            qs_ref[hh * n + i, t:2 * t] = jnp.where(lane < DH_A, zero, q)
    m_ref[...] = jnp.full_like(m_ref, -jnp.inf)
    l_ref[...] = jnp.zeros_like(l_ref)
    acc_ref[...] = jnp.zeros_like(acc_ref)
    lq = lq_ref[...]
    lam = (jnp.exp(jnp.sum(lq[0:1] * lq[1:2], axis=-1, keepdims=True))
           - jnp.exp(jnp.sum(lq[2:3] * lq[3:4], axis=-1, keepdims=True)) + lam_init)

    def scores(hh, i, j):
        return _dot_nt(qs_ref[hh * n + i], k_ref[0, j * t:(j + 1) * t, lanes(hh)])

    def update(hh, i, j, s):
        x = hh * n + i
        if i == j:
            r_i = lax.broadcasted_iota(jnp.int32, (2 * t, t), 0) % t
            c_i = lax.broadcasted_iota(jnp.int32, (2 * t, t), 1)
            s = jnp.where(c_i <= r_i, s, -jnp.inf)
        m_old = m_ref[x]
        m_new = jnp.maximum(m_old, jnp.max(s, axis=-1, keepdims=True))
        alpha = jnp.exp2(m_old - m_new)
        p = jnp.exp2(s - jnp.concatenate([m_new] * (t // LANES), axis=1))
        l_ref[x] = alpha * l_ref[x] + jnp.sum(p, axis=-1, keepdims=True)
        acc_ref[x] = alpha * acc_ref[x] + _dot(p.astype(BF16), v_ref[0, j * t:(j + 1) * t, lanes(hh)])
        m_ref[x] = m_new
        if i == j:
            o = acc_ref[x] / l_ref[x]
            o = o[0:t] - lam * o[t:2 * t]
            ms = jnp.mean(o * o, axis=-1, keepdims=True)
            o_ref[0, i * t:(i + 1) * t, lanes(hh)] = (
                o * lax.rsqrt(ms + EPS) * sw_ref[...] * (1.0 - lam_init)).astype(BF16)

    chains = []
    for hh in range(n_heads):
        pair = [[], []]
        for a in range((n + 1) // 2):
            tiles = [a] if n - 1 - a == a else [a, n - 1 - a]
            shorter = pair[0] if len(pair[0]) <= len(pair[1]) else pair[1]
            shorter += [(hh, i, j) for i in tiles for j in range(i + 1)]
        chains += pair
    order = []
    for x in range(max(len(ch) for ch in chains)):
        for ch in chains:
            order += ch[x:x + 1]
    s_next = scores(*order[0])
    for x, step in enumerate(order):
        s_cur = s_next
        if x + 1 < len(order):
            s_next = scores(*order[x + 1])
        update(*step, s_cur)


def _diff_attention(mix, lambda_qk, subln_w, layer_idx):
    B, S, _ = mix.shape
    t = min(ATT_T, S)
    n = S // t
    nh = ATT_HEADS
    width = nh * LANES
    lam_init = 0.8 - 0.6 * math.exp(-0.3 * layer_idx)
    spec = lambda off: pl.BlockSpec((1, S, width), lambda b, h: (b, 0, off // width + h))
    return pl.pallas_call(
        functools.partial(_attn_kernel, lam_init, t, nh),
        grid=(B, H_A // nh),
        in_specs=[spec(_M_A), spec(_M_A + W_A), spec(_M_A + 2 * W_A),
                  _pick_spec(lambda_qk), _pick_spec(subln_w)],
        out_specs=pl.BlockSpec((1, S, width), lambda b, h: (b, 0, h)),
        out_shape=jax.ShapeDtypeStruct((B, S, W_A), BF16),
        scratch_shapes=[pltpu.VMEM((nh * n, 2 * t, LANES), BF16), pltpu.VMEM((nh * n, 2 * t, LANES), F32),
                        pltpu.VMEM((nh * n, 2 * t, LANES), F32), pltpu.VMEM((nh * n, 2 * t, LANES), F32)],
        compiler_params=_cparams(2),
        name="diff_attention",
    )(mix, mix, mix, lambda_qk[0], subln_w[0])


def _rwkv_kernel(r_ref, k_ref, v_ref, kk_ref, beta_ref, lw_ref, g_ref, bonus_ref, lnw_ref, lnb_ref, g64_ref,
                 o_ref, state_ref):
    C = RWKV_C
    C2 = 2 * C

    @pl.when(pl.program_id(1) == 0)
    def _():
        state_ref[...] = jnp.zeros_like(state_ref)

    lane = lax.broadcasted_iota(jnp.int32, (1, LANES), 1)
    head0 = lane < N_B
    ri = lax.broadcasted_iota(jnp.int32, (C, 1), 0)
    ci = lax.broadcasted_iota(jnp.int32, (1, C2), 1) % C
    strict = ci < ri
    incl = ci <= ri
    same_state = (lax.broadcasted_iota(jnp.int32, (LANES, 1), 0) < N_B) == head0

    def stack(x):
        z = jnp.zeros_like(x)
        return jnp.concatenate([jnp.where(head0, x, z), jnp.where(head0, z, x)], axis=0)

    n_pairs = H_B // 2
    n_seq = r_ref.shape[0]

    def st_prep(g):
        bi = g["bi"]
        lw = lw_ref[bi]
        cum = _cumsum_rows(lw, C)
        c_last = cum[C - 1:C]
        p_inv = jnp.exp(-cum)
        p_end = jnp.exp(c_last - cum)
        p_last = jnp.exp(c_last)
        k = k_ref[bi].astype(F32)
        beta = beta_ref[bi].astype(F32)
        v_all = v_ref[bi]
        rt = (r_ref[bi].astype(F32) * jnp.exp(cum)).astype(BF16)
        kt = (kk_ref[bi].astype(F32) * jnp.exp(cum - lw)).astype(BF16)
        kh = (k * p_inv).astype(BF16)
        bh = (beta * p_inv).astype(BF16)
        ke = (k * p_end).astype(BF16)
        be = (beta * p_end).astype(BF16)
        g["chains"] = []
        for p in range(n_pairs):
            sl = slice(p * LANES, (p + 1) * LANES)
            v = v_all[:, sl]
            g["chains"].append(dict(
                ktrt=jnp.concatenate([kt[:, sl], rt[:, sl]], axis=0),
                rhs4=jnp.concatenate([stack(kh[:, sl]), stack(bh[:, sl])], axis=0),
                v=v, v_s=stack(v),
                kebe=jnp.concatenate([ke[:, sl], -be[:, sl]], axis=0),
                p_last=p_last[:, sl]))

    def st_a(g):
        for d in g["chains"]:
            a_all = _dot_nt(d["ktrt"], d.pop("rhs4"))
            d["a_kk"] = jnp.where(strict, a_all[0:C, 0:C2], 0.0).astype(BF16)
            d["a_o"] = jnp.concatenate([jnp.where(incl, a_all[C:, 0:C2], 0.0),
                                        -jnp.where(incl, a_all[C:, C2:], 0.0)], axis=1).astype(BF16)
            d["xm"] = -jnp.where(strict, a_all[0:C, C2:], 0.0)

    def st_p1(g):
        for d in g["chains"]:
            pw = d["xm"].astype(BF16)
            d["pf"] = _dot(pw, stack(pw))

    def st_neumann(s):
        def stage(g):
            for d in g["chains"]:
                pf = d["pf"]
                pw = pf.astype(BF16)
                xs = stack(d["xm"].astype(BF16))
                if 2 * s >= C:
                    d["xm"] = d["xm"] + pf + _dot(pw, xs)
                else:
                    both = _dot(pw, jnp.concatenate([stack(pw), xs], axis=1))
                    d["xm"] = d["xm"] + pf + both[:, C2:]
                    d["pf"] = both[:, 0:C2]
        return stage

    def st_state_in(g):
        for p, d in enumerate(g["chains"]):
            d["av"] = _dot(d.pop("a_kk"), d["v_s"])
            d["xm"] = d["xm"].astype(BF16)
            d["st"] = state_ref[g["bi"] * n_pairs + p]
            d["w1r1"] = _dot_nt(d["ktrt"], d["st"].astype(BF16))

    def st_u(g):
        for d in g["chains"]:
            rhs = d["w1r1"][0:C] + d["av"]
            d["u"] = (rhs + _dot(d["xm"], stack(rhs.astype(BF16)))).astype(BF16)

    def st_out(g):
        outs = []
        for p, d in enumerate(g["chains"]):
            u = d["u"]
            outs.append(d["w1r1"][C:C2] + _dot(d["a_o"], jnp.concatenate([d["v_s"], stack(u)], axis=0)))
            upd = _dot_tn(jnp.concatenate([d["v"], u], axis=0), d["kebe"])
            state_ref[g["bi"] * n_pairs + p] = d["st"] * d["p_last"] + jnp.where(same_state, upd, 0.0)
        g["o"] = jnp.concatenate(outs, axis=1)

    def st_center(g):
        g["d"] = g["o"] - _group_sum(g["o"], g64_ref, True) * (1.0 / N_B)

    def st_norm(g):
        bi, d = g["bi"], g["d"]
        var = _group_sum(d * d, g64_ref, False) * (1.0 / N_B)
        y = d * lax.rsqrt(var + RWKV_LN_EPS) * lnw_ref[...] + lnb_ref[...] + bonus_ref[bi].astype(F32)
        o_ref[bi] = (y * g_ref[bi].astype(F32)).astype(BF16)

    stages = [st_prep, st_a, st_p1]
    s = 2
    while s < C:
        stages.append(st_neumann(s))
        s *= 2
    stages += [st_state_in, st_u, st_out, st_center, st_norm]
    groups = [dict(bi=bi) for bi in range(n_seq)]
    for tau in range(len(stages) + len(groups) - 1):
        for gi, g in enumerate(groups):
            if 0 <= tau - gi < len(stages):
                stages[tau - gi](g)


def _rwkv_scan(mix, logs, ln_w, ln_b, g64):
    B, S, _ = mix.shape
    W = W_B
    t = RWKV_C
    nb = RWKV_NB if B % RWKV_NB == 0 else 1
    tok = lambda j: pl.BlockSpec((nb, t, W), lambda b, c: (b, c, j))
    m0 = _M_B // W
    return pl.pallas_call(
        _rwkv_kernel,
        grid=(B // nb, S // t),
        in_specs=[tok(m0 + j) for j in range(5)] + [tok(0), tok(m0 + 5), tok(m0 + 6)]
                 + [_pick_spec(ln_w), _pick_spec(ln_b), _pick_spec(g64)],
        out_specs=tok(0),
        out_shape=jax.ShapeDtypeStruct((B, S, W), BF16),
        scratch_shapes=[pltpu.VMEM((nb * H_B // 2, LANES, LANES), F32)],
        compiler_params=_cparams(2),
        name="rwkv7_scan",
    )(mix, mix, mix, mix, mix, logs, mix, mix, ln_w[0], ln_b[0], g64[0])


def _hgrn_kernel(q_ref, k_ref, v_ref, lf_ref, gate_ref, nw_ref, o_ref, state_ref):
    C = HGRN_C
    c = HGRN_SUB
    n_chunks = q_ref.shape[1] // C

    @pl.when(pl.program_id(1) == 0)
    def _():
        state_ref[...] = jnp.zeros_like(state_ref)

    lane_c = lax.broadcasted_iota(jnp.int32, (1, c), 1)
    row_c = lax.broadcasted_iota(jnp.int32, (c, 1), 0)

    units = []
    for n in range(n_chunks):
        rows = slice(n * C, (n + 1) * C)
        b = _cumsum_rows(lf_ref[0, rows, :], C) * math.log2(math.e)
        b_end = b[C - 1:C]
        q = q_ref[0, rows, :].astype(F32)
        k = k_ref[0, rows, :].astype(F32)
        v_all = v_ref[0, rows, :]
        q_in = (q * jnp.exp2(b)).astype(BF16)
        k_end = (k * jnp.exp2(b_end - b)).astype(BF16)
        p_end = jnp.exp2(b_end)
        for h in range(H_C):
            sl = slice(h * LANES, (h + 1) * LANES)
            units.append(dict(b=b[:, sl], q=q[:, sl], k=k[:, sl], v=v_all[:, sl], q_in=q_in[:, sl],
                              k_end=k_end[:, sl], p_end=p_end[:, sl]))

    for d in units:
        d["cross"] = []
        for i in range(1, C // c):
            r0 = i * c
            b_ref_ = d["b"][r0 - 1:r0]
            qs = (d["q"][r0:r0 + c] * jnp.exp2(d["b"][r0:r0 + c] - b_ref_)).astype(BF16)
            ks = (d["k"][0:r0] * jnp.exp2(b_ref_ - d["b"][0:r0])).astype(BF16)
            d["cross"].append(_dot_nt(qs, ks).astype(BF16))
    for d in units:
        d["diag"] = []
        for i in range(C // c):
            r0 = i * c
            b_i = d["b"][r0:r0 + c]
            q_i = d["q"][r0:r0 + c]
            k_i = d["k"][r0:r0 + c]
            sc = jnp.zeros((c, c), F32)
            for j in range(c):
                e = jnp.exp2(b_i - b_i[j:j + 1])
                col = jnp.sum(q_i * e * k_i[j:j + 1], axis=-1, keepdims=True)
                sc = jnp.where(lane_c == j, col, sc)
            d["diag"].append(jnp.where(lane_c <= row_c, sc, 0.0).astype(BF16))
    for d in units:
        blocks = []
        for i in range(C // c):
            r0 = i * c
            o_i = _dot(d["diag"][i], d["v"][r0:r0 + c])
            if i > 0:
                o_i = o_i + _dot(d["cross"][i - 1], d["v"][0:r0])
            blocks.append(o_i)
        d["intra"] = jnp.concatenate(blocks, axis=0)

    state = [state_ref[h] for h in range(H_C)]
    o_chunks = []
    for n in range(n_chunks):
        us = units[n * H_C:(n + 1) * H_C]
        o_heads = [_dot_nt(d["q_in"], st.astype(BF16)) + d["intra"] for d, st in zip(us, state)]
        state = [st * d["p_end"] + _dot_tn(d["v"], d["k_end"]) for d, st in zip(us, state)]
        o_chunks.append(jnp.concatenate(
            [o_h * lax.rsqrt(jnp.mean(o_h * o_h, axis=-1, keepdims=True) + EPS) * nw_ref[...] for o_h in o_heads],
            axis=1))
    for h in range(H_C):
        state_ref[h] = state[h]
    o = jnp.concatenate(o_chunks, axis=0)
    o_ref[0] = (o * gate_ref[0].astype(F32)).astype(BF16)


def _hgrn_scan(mix, logs, norm_w):
    B, S, _ = mix.shape
    W = W_C
    t = min(HGRN_T, S)
    tok = lambda j: pl.BlockSpec((1, t, W), lambda b, c: (b, c, j))
    m0 = _M_C // W
    return pl.pallas_call(
        _hgrn_kernel,
        grid=(B, S // t),
        in_specs=[tok(m0), tok(m0 + 1), tok(m0 + 2), tok(W_B // W), tok(m0 + 3), _pick_spec(norm_w)],
        out_specs=tok(0),
        out_shape=jax.ShapeDtypeStruct((B, S, W), BF16),
        scratch_shapes=[pltpu.VMEM((H_C, LANES, LANES), F32)],
        compiler_params=_cparams(2),
        name="hgrn2_scan",
    )(mix, mix, mix, logs, mix, norm_w[0])


def _merge_kernel(h_ref, mod_ref, oa_ref, ob_ref, oc_ref, gates_ref, wa_ref, wb_ref, wc_ref, wo_ref, o_ref):
    D = h_ref.shape[2]
    gate = lambda j: _sigmoid(gates_ref[0, :, j * D:(j + 1) * D].astype(F32))
    z = (gate(0) * _dot(oa_ref[0], wa_ref[...])
         + gate(1) * _dot(ob_ref[0], wb_ref[...])
         + gate(2) * _dot(oc_ref[0], wc_ref[...]))
    y = _dot(z.astype(BF16), wo_ref[...])
    o_ref[0] = h_ref[0] + mod_ref[0][5:6] * y


def _merge(h, mod, l, oa, ob, oc, mix, wa, wb, wc, wo):
    B, S, D = h.shape
    tm = min(MERGE_TM, S)
    tok = lambda width: pl.BlockSpec((1, tm, width), lambda b, s: (b, s, 0))
    return pl.pallas_call(
        _merge_kernel,
        grid=(B, S // tm),
        in_specs=[tok(D), _mod_spec(mod, l), tok(W_A), tok(W_B), tok(W_C), tok(3 * D),
                  _pick_spec(wa), _pick_spec(wb), _pick_spec(wc), _pick_spec(wo)],
        out_specs=tok(D),
        out_shape=jax.ShapeDtypeStruct((B, S, D), F32),
        compiler_params=_cparams(2),
        name="merge",
    )(h, mod, oa, ob, oc, mix, wa[0], wb[0], wc[0], wo[0])


def _block_ones(width, group):
    i = jnp.arange(width) // group
    return (i[:, None] == i[None, :]).astype(BF16)


def kernel(x, c, positions, mod_w, mod_b, norm_w, ffn_w_gate, ffn_w_up, ffn_w_down, w_in, qk_norm_w, lambda_qk,
           subln_w, w_out_a, rwkv_mu, rwkv_w0, rwkv_w2, rwkv_a0, rwkv_a2, rwkv_g2, rwkv_k_k, rwkv_k_a, rwkv_r_k,
           rwkv_ln_w, rwkv_ln_b, w_out_b, hgrn_lower_bounds, hgrn_norm_w, w_out_c, w_out):
    B, S, D = x.shape
    L = mod_w.shape[0]
    lb = jnp.cumsum(jax.nn.softmax(hgrn_lower_bounds.astype(F32), axis=0), axis=0)
    lb = lb - lb[0]
    log_lb = jnp.log(lb)
    log1m_lb = jnp.log1p(-lb)

    mod = _modulation(c, mod_w, mod_b).reshape(L, B, N_MOD, D)
    cos, sin = _rope_tables(positions)

    pad = R_G_PAD - R_G
    vec = lambda a: a.astype(F32).reshape(L, 1, -1)
    nw4 = norm_w.reshape(L, 3, 1, D)
    wg, wu, wd = ffn_w_gate.astype(BF16), ffn_w_up.astype(BF16), ffn_w_down.astype(BF16)
    w_ab = jnp.pad(w_in[:, :, :P_A + P_B].astype(BF16), ((0, 0), (0, 0), (0, pad)))
    w_cg = w_in[:, :, P_A + P_B:].astype(BF16)
    g64 = _pick(_block_ones(2 * LANES, N_B))
    mix_params = [
        jnp.tile(qk_norm_w, (1, 1, W_A // DH_A)), None, jnp.pad(rwkv_mu, ((0, 0), (0, pad)))[:, None, :],
        vec(rwkv_w0), jnp.pad(rwkv_w2, ((0, 0), (0, R_A), (0, 0))).astype(BF16),
        vec(rwkv_a0), jnp.pad(rwkv_a2, ((0, 0), (R_W, 0), (0, 0))).astype(BF16),
        jnp.pad(rwkv_g2, ((0, 0), (0, pad), (0, 0))).astype(BF16),
        vec(rwkv_k_k), vec(rwkv_k_a), vec(rwkv_r_k), vec(log_lb), vec(log1m_lb)]
    ln_w, ln_b, subln, hnw = vec(rwkv_ln_w), vec(rwkv_ln_b), vec(subln_w), vec(hgrn_norm_w)
    wa, wb, wc, wo = (a.astype(BF16) for a in (w_out_a, w_out_b, w_out_c, w_out))

    h = x
    for l in range(L):
        ffn_w = lambda j: (_pick(wg, l, j), _pick(wu, l, j), _pick(wd, l, j))
        h = _ffn(h, mod, l, _pick(nw4, l, 0), *ffn_w(0), 0)
        params = ([_pick(nw4, l, 1), _pick(w_ab, l), _pick(w_cg, l)]
                  + [g64 if a is None else _pick(a, l) for a in mix_params])
        mix, logs = _mixer_in(h, mod, l, cos, sin, params)
        oa = _diff_attention(mix, _pick(lambda_qk, l), _pick(subln, l), l)
        ob = _rwkv_scan(mix, logs, _pick(ln_w, l), _pick(ln_b, l), g64)
        oc = _hgrn_scan(mix, logs, _pick(hnw, l))
        h = _merge(h, mod, l, oa, ob, oc, mix, _pick(wa, l), _pick(wb, l), _pick(wc, l), _pick(wo, l))
        h = _ffn(h, mod, l, _pick(nw4, l, 2), *ffn_w(1), 6)
    return h
```
